```python
import jax, jax.numpy as jnp
from jax import lax
import numpy as np

D_MODEL = 1024
BATCH = 4
SEQ = 4096
DEPTH = 1

MEM_LEN = 256
MOBA_HEADS = D_MODEL // 128
MOBA_HEAD_DIM = 64
MOBA_WIDTH = MOBA_HEADS * MOBA_HEAD_DIM
MOBA_BLOCK = 256
MOBA_TOPK = 3
MOBA_Q_CHUNK = 32
GLA_HEADS = 4
GLA_VAL_DIM = (D_MODEL // 2) // GLA_HEADS
GLA_KEY_DIM = GLA_VAL_DIM // 2
GLA_K_WIDTH = GLA_HEADS * GLA_KEY_DIM
GLA_V_WIDTH = GLA_HEADS * GLA_VAL_DIM
GLA_GATE_RANK = 16
GLA_GATE_TEMP = 16.0
GLA_CHUNK = 64
MIX_WIDTH = MOBA_WIDTH + GLA_V_WIDTH
IN_SPLITS = (MOBA_WIDTH, MOBA_WIDTH, MOBA_WIDTH, GLA_K_WIDTH, GLA_K_WIDTH, GLA_V_WIDTH, GLA_V_WIDTH, GLA_GATE_RANK)
IN_WIDTH = 3 * MOBA_WIDTH + 2 * GLA_K_WIDTH + 2 * GLA_V_WIDTH + GLA_GATE_RANK
XATTN_HEADS = 4
XATTN_HEAD_DIM = D_MODEL // XATTN_HEADS
D_FF = 4 * D_MODEL
LN_EPS = 1e-5
RMS_EPS = 1e-6
DEEPNORM_ALPHA = (2.0 * DEPTH) ** 0.25
DEEPNORM_BETA = (8.0 * DEPTH) ** -0.25

kernel_name = "hymba_moba_gla_deepnorm_layer"


def _heads(t, n_heads):
    b, s, _ = t.shape
    return t.reshape(b, s, n_heads, -1).transpose(0, 2, 1, 3)


def _merge(t):
    b, h, s, d = t.shape
    return t.transpose(0, 2, 1, 3).reshape(b, s, h * d)


def _layer_norm(x, g, b):
    xf = x.astype(jnp.float32)
    mu = jnp.mean(xf, axis=-1, keepdims=True)
    var = jnp.mean(jnp.square(xf - mu), axis=-1, keepdims=True)
    return ((xf - mu) * lax.rsqrt(var + LN_EPS) * g + b).astype(x.dtype)


def _alibi_slopes(n_heads):
    return 2.0 ** (-8.0 * jnp.arange(1, n_heads + 1, dtype=jnp.float32) / n_heads)


def _moba(q, k, v):
    B, H, S, dh = q.shape
    s_pad = -(-S // MOBA_BLOCK) * MOBA_BLOCK
    pad = [(0, 0), (0, 0), (0, s_pad - S), (0, 0)]
    q, k, v = jnp.pad(q, pad), jnp.pad(k, pad), jnp.pad(v, pad)
    nb = s_pad // MOBA_BLOCK
    topk = min(MOBA_TOPK, nb)
    scale = dh ** -0.5
    slopes = _alibi_slopes(H)
    kb = k.reshape(B, H, nb, MOBA_BLOCK, dh)
    vb = v.reshape(B, H, nb, MOBA_BLOCK, dh)
    kmean = jnp.mean(kb.astype(jnp.float32), axis=3)
    gate = jnp.einsum('bhsd,bhnd->bhsn', q.astype(jnp.float32), kmean)
    qblk = jnp.arange(s_pad) // MOBA_BLOCK
    past = jnp.arange(nb)[None, :] < qblk[:, None]
    gate = jnp.where(past, gate, -jnp.inf)
    gval, gidx = lax.top_k(gate, topk)
    gvalid = jnp.isfinite(gval)
    bi = jnp.arange(B)[:, None, None, None]
    hi = jnp.arange(H)[None, :, None, None]
    blk_off = jnp.arange(MOBA_BLOCK)

    def chunk(c):
        t0 = c * MOBA_Q_CHUNK
        qc = lax.dynamic_slice_in_dim(q, t0, MOBA_Q_CHUNK, axis=2)
        idx = lax.dynamic_slice_in_dim(gidx, t0, MOBA_Q_CHUNK, axis=2)
        valid = lax.dynamic_slice_in_dim(gvalid, t0, MOBA_Q_CHUNK, axis=2)
        tpos = t0 + jnp.arange(MOBA_Q_CHUNK)
        blk = t0 // MOBA_BLOCK
        k_own = lax.dynamic_index_in_dim(kb, blk, axis=2, keepdims=False)
        v_own = lax.dynamic_index_in_dim(vb, blk, axis=2, keepdims=False)
        dist_own = tpos[:, None] - (blk * MOBA_BLOCK + blk_off)[None, :]
        s_own = jnp.einsum('bhqd,bhkd->bhqk', qc, k_own).astype(jnp.float32) * scale
        s_own = s_own - slopes[:, None, None] * dist_own.astype(jnp.float32)
        s_own = jnp.where(dist_own >= 0, s_own, -jnp.inf)
        k_sel = kb[bi, hi, idx]
        v_sel = vb[bi, hi, idx]
        dist_sel = tpos[:, None, None] - (idx[..., None] * MOBA_BLOCK + blk_off)
        s_sel = jnp.einsum('bhqd,bhqjkd->bhqjk', qc, k_sel).astype(jnp.float32) * scale
        s_sel = s_sel - slopes[None, :, None, None, None] * dist_sel.astype(jnp.float32)
        s_sel = jnp.where(valid[..., None], s_sel, -jnp.inf)
        s_all = jnp.concatenate([s_own, s_sel.reshape(B, H, MOBA_Q_CHUNK, topk * MOBA_BLOCK)], axis=-1)
        p = jax.nn.softmax(s_all, axis=-1).astype(v.dtype)
        p_own = p[..., :MOBA_BLOCK]
        p_sel = p[..., MOBA_BLOCK:].reshape(B, H, MOBA_Q_CHUNK, topk, MOBA_BLOCK)
        o = jnp.einsum('bhqk,bhkd->bhqd', p_own, v_own) + jnp.einsum('bhqjk,bhqjkd->bhqd', p_sel, v_sel)
        return o

    out = lax.map(chunk, jnp.arange(s_pad // MOBA_Q_CHUNK))
    out = out.transpose(1, 2, 0, 3, 4).reshape(B, H, s_pad, dh)
    return out[:, :, :S].astype(q.dtype)


def _gla(q, k, v, log_a):
    B, H, S, dk = q.shape
    dv = v.shape[-1]
    C = GLA_CHUNK
    n = S // C

    def to_chunks(t):
        return t.astype(jnp.float32).reshape(B, H, n, C, t.shape[-1]).transpose(2, 0, 1, 3, 4)

    xs = (to_chunks(q * (dk ** -0.5)), to_chunks(k), to_chunks(v), to_chunks(log_a))
    causal = jnp.tril(jnp.ones((C, C), dtype=bool))

    def step(state, inp):
        qc, kc, vc, gc = inp
        b = jnp.cumsum(gc, axis=2)
        o_inter = jnp.einsum('bhtd,bhde->bhte', qc * jnp.exp(b), state)
        diff = b[:, :, :, None, :] - b[:, :, None, :, :]
        decay = jnp.exp(jnp.where(causal[:, :, None], diff, -jnp.inf))
        attn = jnp.einsum('bhtd,bhsd,bhtsd->bhts', qc, kc, decay)
        o_intra = jnp.einsum('bhts,bhse->bhte', attn, vc)
        b_last = b[:, :, -1:, :]
        new_state = jnp.exp(b_last[:, :, 0, :])[..., None] * state + jnp.einsum('bhsd,bhse->bhde', kc * jnp.exp(b_last - b), vc)
        return new_state, o_inter + o_intra

    init = jnp.zeros((B, H, dk, dv), jnp.float32)
    _, out = lax.scan(step, init, xs)
    return out.transpose(1, 2, 0, 3, 4).reshape(B, H, S, dv)


def _hybrid_mixer(h, w_in, w_gate_up, b_gate, gla_norm_g, w_mix_o):
    proj = h @ w_in
    cuts = np.cumsum(IN_SPLITS)[:-1].tolist()
    qa, ka, va, qg, kg, vg, rg, glr = jnp.split(proj, cuts, axis=-1)
    moba_o = _merge(_moba(_heads(qa, MOBA_HEADS), _heads(ka, MOBA_HEADS), _heads(va, MOBA_HEADS)))
    log_a = jax.nn.log_sigmoid((glr @ w_gate_up + b_gate).astype(jnp.float32)) / GLA_GATE_TEMP
    o = _gla(_heads(qg, GLA_HEADS), _heads(kg, GLA_HEADS), _heads(vg, GLA_HEADS), _heads(log_a, GLA_HEADS))
    o = o * lax.rsqrt(jnp.mean(jnp.square(o), axis=-1, keepdims=True) + RMS_EPS) * gla_norm_g
    gla_o = (_merge(o) * jax.nn.silu(rg.astype(jnp.float32))).astype(h.dtype)
    mix = jnp.concatenate([moba_o, gla_o], axis=-1)
    return mix @ w_mix_o


def _mem_xattn(h, mem, w_xq, w_xkv, w_xo):
    q = _heads(h @ w_xq, XATTN_HEADS)
    k, v = jnp.split(mem @ w_xkv, 2, axis=-1)
    k, v = _heads(k, XATTN_HEADS), _heads(v, XATTN_HEADS)
    s = jnp.einsum('bhsd,bhmd->bhsm', q, k).astype(jnp.float32) * (XATTN_HEAD_DIM ** -0.5)
    p = jax.nn.softmax(s, axis=-1).astype(v.dtype)
    return _merge(jnp.einsum('bhsm,bhmd->bhsd', p, v)) @ w_xo


def _sq_relu_mlp(h, w_ff1, w_ff2):
    return jnp.square(jax.nn.relu(h @ w_ff1)) @ w_ff2


def setup_inputs(seed: int = 0) -> dict:
    key = jax.random.key(seed)
    ks = jax.random.split(key, 32)
    f32 = jnp.float32

    def nrm(k, shape, fan_in, gain=1.0):
        return jax.random.normal(k, shape, f32) * (gain * fan_in ** -0.5)

    L = DEPTH
    b = DEEPNORM_BETA
    in_gains = (1.0, 1.0, b, 1.0, 1.0, b, 1.0, 1.0)
    pieces = [nrm(ks[10 + i], (L, D_MODEL, w), D_MODEL, g) for i, (w, g) in enumerate(zip(IN_SPLITS, in_gains))]
    w_in = jnp.concatenate(pieces, axis=-1)
    xkv = jnp.concatenate([nrm(ks[20], (L, D_MODEL, D_MODEL), D_MODEL), nrm(ks[21], (L, D_MODEL, D_MODEL), D_MODEL, b)], axis=-1)
    return {
        "x": jax.random.normal(ks[0], (BATCH, SEQ, D_MODEL), f32),
        "mem": jax.random.normal(ks[1], (BATCH, MEM_LEN, D_MODEL), f32),
        "w_in": w_in,
        "w_gate_up": nrm(ks[2], (L, GLA_GATE_RANK, GLA_K_WIDTH), GLA_GATE_RANK),
        "b_gate": 0.1 * jax.random.normal(ks[3], (L, GLA_K_WIDTH), f32),
        "gla_norm_g": 1.0 + 0.02 * jax.random.normal(ks[4], (L, GLA_VAL_DIM), f32),
        "w_mix_o": nrm(ks[5], (L, MIX_WIDTH, D_MODEL), MIX_WIDTH, b),
        "ln1_g": 1.0 + 0.02 * jax.random.normal(ks[6], (L, D_MODEL), f32),
        "ln1_b": 0.02 * jax.random.normal(ks[7], (L, D_MODEL), f32),
        "w_xq": nrm(ks[8], (L, D_MODEL, D_MODEL), D_MODEL),
        "w_xkv": xkv,
        "w_xo": nrm(ks[9], (L, D_MODEL, D_MODEL), D_MODEL, b),
        "ln2_g": 1.0 + 0.02 * jax.random.normal(ks[22], (L, D_MODEL), f32),
        "ln2_b": 0.02 * jax.random.normal(ks[23], (L, D_MODEL), f32),
        "w_ff1": nrm(ks[24], (L, D_MODEL, D_FF), D_MODEL, b),
        "w_ff2": nrm(ks[25], (L, D_FF, D_MODEL), D_FF, b),
        "ln3_g": 1.0 + 0.02 * jax.random.normal(ks[26], (L, D_MODEL), f32),
        "ln3_b": 0.02 * jax.random.normal(ks[27], (L, D_MODEL), f32),
    }


def reference(x, mem, w_in, w_gate_up, b_gate, gla_norm_g, w_mix_o, ln1_g, ln1_b, w_xq, w_xkv, w_xo, ln2_g, ln2_b, w_ff1, w_ff2, ln3_g, ln3_b):
    for l in range(DEPTH):
        mix = _hybrid_mixer(x, w_in[l], w_gate_up[l], b_gate[l], gla_norm_g[l], w_mix_o[l])
        x = _layer_norm(DEEPNORM_ALPHA * x + mix, ln1_g[l], ln1_b[l])
        xa = _mem_xattn(x, mem, w_xq[l], w_xkv[l], w_xo[l])
        x = _layer_norm(DEEPNORM_ALPHA * x + xa, ln2_g[l], ln2_b[l])
        ff = _sq_relu_mlp(x, w_ff1[l], w_ff2[l])
        x = _layer_norm(DEEPNORM_ALPHA * x + ff, ln3_g[l], ln3_b[l])
    return x
```

```python
import functools

import jax
import jax.numpy as jnp
from jax import lax
from jax.experimental import pallas as pl
from jax.experimental.pallas import tpu as pltpu

F32 = jnp.float32
BF16 = jnp.bfloat16

MOBA_HEAD_DIM = 64
MOBA_BLOCK = 256
MOBA_TOPK = 3
GLA_HEADS = 4
GLA_KEY_DIM = 64
GLA_VAL_DIM = 128
GLA_GATE_RANK = 16
GLA_GATE_TEMP = 16.0
GLA_CHUNK = 64
XATTN_HEADS = 4
LN_EPS = 1e-5
RMS_EPS = 1e-6

LANES = 128
VMEM_LIMIT_BYTES = 56 * 1024 * 1024

NEG_INF = float("-inf")


def _dot(a, b):
    return jnp.dot(a, b, preferred_element_type=F32)


def _dot_nt(a, b):
    return lax.dot_general(a, b, (((1,), (1,)), ((), ())), preferred_element_type=F32)


def _dot_tn(a, b):
    return lax.dot_general(a, b, (((0,), (0,)), ((), ())), preferred_element_type=F32)


def _split3(a):
    hi = a.astype(BF16)
    r1 = a - hi.astype(F32)
    mid = r1.astype(BF16)
    lo = (r1 - mid.astype(F32)).astype(BF16)
    return hi, mid, lo


def _layer_norm(x, g, b):
    mu = jnp.mean(x, axis=-1, keepdims=True)
    xc = x - mu
    var = jnp.mean(xc * xc, axis=-1, keepdims=True)
    return xc * lax.rsqrt(var + LN_EPS) * g + b


def _log_sigmoid(z):
    return jnp.minimum(z, 0.0) - jnp.log(1.0 + jnp.exp(-jnp.abs(z)))


def _sigmoid(z):
    return 1.0 / (1.0 + jnp.exp(-z))


def _inproj_kernel(x_ref, wqk_ref, wvat_ref, wg_ref, wvr_ref, wglr_ref, wgu_ref, bg_ref,
                   qa_ref, ka_ref, vat_ref, qg_ref, kg_ref, vg_ref, rg_ref, la_ref):
    xb = x_ref[0].astype(BF16)
    aw = qa_ref.shape[-1]
    qk = _dot(xb, wqk_ref[...])
    qa_ref[0] = (qk[:, :aw] * (MOBA_HEAD_DIM ** -0.5)).astype(BF16)
    ka_ref[0] = qk[:, aw:].astype(BF16)
    vat_ref[0] = _dot_nt(wvat_ref[...], xb).astype(BF16)
    kw = qg_ref.shape[-1]
    g = _dot(xb, wg_ref[...])
    qg_ref[0] = g[:, :kw] * (GLA_KEY_DIM ** -0.5)
    kg_ref[0] = g[:, kw:]
    vw = vg_ref.shape[-1]
    vr = _dot(xb, wvr_ref[...])
    vg_ref[0] = vr[:, :vw].astype(BF16)
    rg_ref[0] = vr[:, vw:]
    glr = _dot(xb, wglr_ref[...])
    gh, gm, gl = _split3(glr)
    wh, wm, wl = _split3(wgu_ref[...])
    z = (_dot(gh, wh) + _dot(gm, wh) + _dot(gh, wm)
         + _dot(gl, wh) + _dot(gm, wm) + _dot(gh, wl)) + bg_ref[...]
    la_ref[0] = _log_sigmoid(z) / GLA_GATE_TEMP


def _in_projection(x, w_in, w_gate_up, b_gate, tm=512):
    B, S, D = x.shape
    aw = 512
    kw = GLA_HEADS * GLA_KEY_DIM
    vw = GLA_HEADS * GLA_VAL_DIM
    c = 0
    wq = w_in[:, c:c + aw]; c += aw
    wk = w_in[:, c:c + aw]; c += aw
    wv = w_in[:, c:c + aw]; c += aw
    wqg = w_in[:, c:c + kw]; c += kw
    wkg = w_in[:, c:c + kw]; c += kw
    wvg = w_in[:, c:c + vw]; c += vw
    wrg = w_in[:, c:c + vw]; c += vw
    wglr = w_in[:, c:c + GLA_GATE_RANK]
    wqk = jnp.concatenate([wq, wk], axis=1).astype(BF16)
    wvat = wv.T.astype(BF16)
    wg = jnp.concatenate([wqg, wkg], axis=1).astype(BF16)
    wvr = jnp.concatenate([wvg, wrg], axis=1).astype(BF16)
    wglr_p = jnp.pad(wglr, ((0, 0), (0, LANES - GLA_GATE_RANK))).astype(BF16)
    wgu_p = jnp.pad(w_gate_up, ((0, LANES - GLA_GATE_RANK), (0, 0)))
    bg = b_gate.reshape(1, kw)

    def full(a):
        return pl.BlockSpec(a.shape, lambda b, s: (0,) * a.ndim)

    row = lambda w: pl.BlockSpec((1, tm, w), lambda b, s: (b, s, 0))
    out_shape = (
        jax.ShapeDtypeStruct((B, S, aw), BF16),
        jax.ShapeDtypeStruct((B, S, aw), BF16),
        jax.ShapeDtypeStruct((B, aw, S), BF16),
        jax.ShapeDtypeStruct((B, S, kw), F32),
        jax.ShapeDtypeStruct((B, S, kw), F32),
        jax.ShapeDtypeStruct((B, S, vw), BF16),
        jax.ShapeDtypeStruct((B, S, vw), F32),
        jax.ShapeDtypeStruct((B, S, kw), F32),
    )
    out_specs = (row(aw), row(aw), pl.BlockSpec((1, aw, tm), lambda b, s: (b, 0, s)),
                 row(kw), row(kw), row(vw), row(vw), row(kw))
    return pl.pallas_call(
        _inproj_kernel,
        grid=(B, S // tm),
        in_specs=[row(D), full(wqk), full(wvat), full(wg), full(wvr), full(wglr_p), full(wgu_p), full(bg)],
        out_specs=out_specs,
        out_shape=out_shape,
        compiler_params=pltpu.CompilerParams(
            dimension_semantics=("arbitrary", "arbitrary"), vmem_limit_bytes=VMEM_LIMIT_BYTES),
        name="in_projection",
    )(x, wqk, wvat, wg, wvr, wglr_p, wgu_p, bg)


def _moba_kernel(slopes_ref, q_ref, k_ref, vt_ref, o_ref, kmean_ref, sel_ref, acc_ref, m_ref, l_ref):
    hp = pl.program_id(1)
    i = pl.program_id(2)
    bs = MOBA_BLOCK
    nb = k_ref.shape[1] // bs

    @pl.when(i == 0)
    def _():
        for n in range(nb):
            kb = k_ref[0, n * bs:(n + 1) * bs, :].astype(F32)
            kmean_ref[n:n + 1, :] = jnp.sum(kb, axis=0, keepdims=True) * (1.0 / bs)

    q = q_ref[0]
    lane = lax.broadcasted_iota(jnp.int32, (1, LANES), 1)
    head_lanes = (lane < MOBA_HEAD_DIM, lane >= MOBA_HEAD_DIM)
    qh = [jnp.where(hm, q, jnp.zeros_like(q)) for hm in head_lanes]

    km = kmean_ref[...]
    km_hi, km_mid, km_lo = _split3(km)
    nidx = lax.broadcasted_iota(jnp.int32, (nb, bs), 0)
    for h in range(2):
        g = _dot_nt(km_hi, qh[h]) + _dot_nt(km_mid, qh[h]) + _dot_nt(km_lo, qh[h])
        g = jnp.where(nidx < i, g, NEG_INF)
        rank = jnp.zeros((nb, bs), jnp.int32)
        for m in range(nb):
            gm = g[m:m + 1, :]
            beats = (gm > g) | ((gm == g) & (m < nidx))
            rank = rank + beats.astype(jnp.int32)
        sel = (rank < MOBA_TOPK) & (nidx < i)
        sel_ref[h] = jnp.where(sel, 0.0, NEG_INF)

    krow = lax.broadcasted_iota(jnp.int32, (bs, bs), 0)
    qcol = lax.broadcasted_iota(jnp.int32, (bs, bs), 1)
    dist = (qcol - krow).astype(F32)
    slopes = [slopes_ref[hp * 2 + h] for h in range(2)]
    bias = [-slopes[h] * dist for h in range(2)]
    row = lax.broadcasted_iota(jnp.int32, (LANES, 1), 0)
    head_rows = (row < MOBA_HEAD_DIM, row >= MOBA_HEAD_DIM)

    def kv_block(j):
        start = pl.multiple_of(j * bs, bs)
        kj = k_ref[0, pl.ds(start, bs), :]
        vtj = vt_ref[0, :, pl.ds(start, bs)]
        vcat = jnp.concatenate([jnp.where(hr, vtj, jnp.zeros_like(vtj)) for hr in head_rows], axis=1)
        return kj, vcat

    kj, vcat = kv_block(i)
    ps = []
    for h in range(2):
        t = jnp.where(dist >= 0, _dot_nt(kj, qh[h]) + bias[h], NEG_INF)
        m = jnp.max(t, axis=0, keepdims=True)
        p = jnp.exp(t - m)
        m_ref[h:h + 1, :] = m
        l_ref[h:h + 1, :] = jnp.sum(p, axis=0, keepdims=True)
        ps.append(p.astype(BF16))
    acc_ref[...] = _dot(vcat, jnp.concatenate(ps, axis=0))

    def past_block(j, carry):
        kj, vcat = kv_block(j)
        off = ((i - j) * bs).astype(F32)
        ps = []
        alphas = []
        for h in range(2):
            t = _dot_nt(kj, qh[h]) + bias[h]
            mrow = jnp.max(t, axis=0, keepdims=True)
            cs = sel_ref[h, pl.ds(j, 1), :] - slopes[h] * off
            m_old = m_ref[h:h + 1, :]
            m_new = jnp.maximum(m_old, mrow + cs)
            alpha = jnp.exp(m_old - m_new)
            p = jnp.exp(t + (cs - m_new))
            l_ref[h:h + 1, :] = alpha * l_ref[h:h + 1, :] + jnp.sum(p, axis=0, keepdims=True)
            m_ref[h:h + 1, :] = m_new
            ps.append(p.astype(BF16))
            alphas.append(alpha)
        alpha_full = jnp.where(head_rows[0], alphas[0], alphas[1])
        acc_ref[...] = alpha_full * acc_ref[...] + _dot(vcat, jnp.concatenate(ps, axis=0))
        return carry

    lax.fori_loop(0, i, past_block, 0)

    l_full = jnp.where(head_rows[0], l_ref[0:1, :], l_ref[1:2, :])
    o_ref[0] = (acc_ref[...] / l_full).T.astype(BF16)


def _moba(qa, ka, vat, slopes):
    B, S, W = qa.shape
    bs = MOBA_BLOCK
    nb = S // bs
    return pl.pallas_call(
        _moba_kernel,
        grid=(B, W // LANES, nb),
        in_specs=[
            pl.BlockSpec(memory_space=pltpu.SMEM),
            pl.BlockSpec((1, bs, LANES), lambda b, hp, i: (b, i, hp)),
            pl.BlockSpec((1, S, LANES), lambda b, hp, i: (b, 0, hp)),
            pl.BlockSpec((1, LANES, S), lambda b, hp, i: (b, hp, 0)),
        ],
        out_specs=pl.BlockSpec((1, bs, LANES), lambda b, hp, i: (b, i, hp)),
        out_shape=jax.ShapeDtypeStruct((B, S, W), BF16),
        scratch_shapes=[
            pltpu.VMEM((nb, LANES), F32),
            pltpu.VMEM((2, nb, bs), F32),
            pltpu.VMEM((LANES, bs), F32),
            pltpu.VMEM((8, bs), F32),
            pltpu.VMEM((8, bs), F32),
        ],
        compiler_params=pltpu.CompilerParams(
            dimension_semantics=("arbitrary", "arbitrary", "arbitrary"), vmem_limit_bytes=VMEM_LIMIT_BYTES),
        name="moba_attention",
    )(slopes, qa, ka, vat)


def _gla_kernel(q_ref, k_ref, la_ref, v_ref, r_ref, gn_ref, o_ref, st_ref):
    s_idx = pl.program_id(1)
    C = GLA_CHUNK
    tc = q_ref.shape[1]

    @pl.when(s_idx == 0)
    def _():
        st_ref[...] = jnp.zeros_like(st_ref)

    lane = lax.broadcasted_iota(jnp.int32, (1, LANES), 1)
    head_lanes = (lane < GLA_KEY_DIM, lane >= GLA_KEY_DIM)
    ti = lax.broadcasted_iota(jnp.int32, (C, C), 0)
    si = lax.broadcasted_iota(jnp.int32, (C, C), 1)
    causal = si <= ti
    tril = jnp.where(causal, 1.0, 0.0).astype(BF16)
    gn = gn_ref[...]

    def chunk(c, carry):
        rows = pl.ds(pl.multiple_of(c * C, C), C)
        for hp in range(GLA_HEADS // 2):
            lanes = slice(hp * LANES, (hp + 1) * LANES)
            g = la_ref[0, rows, lanes]
            g_hi, g_mid, g_lo = _split3(g)
            b = _dot(tril, g_hi) + _dot(tril, g_mid) + _dot(tril, g_lo)
            qs = q_ref[0, rows, lanes]
            ks = k_ref[0, rows, lanes]
            b_last = b[C - 1:C, :]
            q_dec = qs * jnp.exp(b)
            k_inv = (ks * jnp.exp(-b)).astype(BF16)
            k_dec = ks * jnp.exp(b_last - b)
            st = st_ref[hp]
            st_b = st.astype(BF16)
            new_st = st * jnp.exp(b_last)
            for h in range(2):
                head = hp * 2 + h
                vl = slice(head * GLA_VAL_DIM, (head + 1) * GLA_VAL_DIM)
                qm = jnp.where(head_lanes[h], q_dec, 0.0).astype(BF16)
                o_inter = _dot_nt(qm, st_b)
                attn = jnp.where(causal, _dot_nt(qm, k_inv), 0.0)
                vh = v_ref[0, rows, vl]
                o = o_inter + _dot(attn.astype(BF16), vh)
                ms = jnp.mean(o * o, axis=-1, keepdims=True)
                on = o * lax.rsqrt(ms + RMS_EPS) * gn
                rr = r_ref[0, rows, vl]
                o_ref[0, rows, vl] = (on * (rr * _sigmoid(rr))).astype(BF16)
                km = jnp.where(head_lanes[h], k_dec, 0.0).astype(BF16)
                new_st = new_st + _dot_tn(vh, km)
            st_ref[hp] = new_st
        return carry

    lax.fori_loop(0, tc // C, chunk, 0)


def _gla(qg, kg, la, vg, rg, gnorm, tc=512):
    B, S, kw = qg.shape
    vw = vg.shape[-1]
    row = lambda w: pl.BlockSpec((1, tc, w), lambda b, s: (b, s, 0))
    return pl.pallas_call(
        _gla_kernel,
        grid=(B, S // tc),
        in_specs=[row(kw), row(kw), row(kw), row(vw), row(vw),
                  pl.BlockSpec((1, GLA_VAL_DIM), lambda b, s: (0, 0))],
        out_specs=row(vw),
        out_shape=jax.ShapeDtypeStruct((B, S, vw), BF16),
        scratch_shapes=[pltpu.VMEM((GLA_HEADS // 2, GLA_VAL_DIM, LANES), F32)],
        compiler_params=pltpu.CompilerParams(
            dimension_semantics=("arbitrary", "arbitrary"), vmem_limit_bytes=VMEM_LIMIT_BYTES),
        name="gla_scan",
    )(qg, kg, la, vg, rg, gnorm)


def _memkv_kernel(mem_ref, w_ref, k_ref, v_ref):
    kv = _dot(mem_ref[0].astype(BF16), w_ref[...])
    d = k_ref.shape[-1]
    k_ref[0] = kv[:, :d].astype(BF16)
    v_ref[0] = kv[:, d:].astype(BF16)


def _mem_kv(mem, w_xkv):
    B, M, D = mem.shape
    wb = w_xkv.astype(BF16)
    blk = pl.BlockSpec((1, M, D), lambda b: (b, 0, 0))
    return pl.pallas_call(
        _memkv_kernel,
        grid=(B,),
        in_specs=[blk, pl.BlockSpec(wb.shape, lambda b: (0, 0))],
        out_specs=(blk, blk),
        out_shape=(jax.ShapeDtypeStruct((B, M, D), BF16), jax.ShapeDtypeStruct((B, M, D), BF16)),
        compiler_params=pltpu.CompilerParams(
            dimension_semantics=("arbitrary",), vmem_limit_bytes=VMEM_LIMIT_BYTES),
        name="mem_kv_projection",
    )(mem, wb)


def _post_mixer_kernel(alpha, x_ref, mo_ref, go_ref, wm1_ref, wm2_ref, g1_ref, b1_ref, wxq_ref,
                       km_ref, vm_ref, wxo_ref, g2_ref, b2_ref, o_ref):
    x = x_ref[0]
    mix = _dot(mo_ref[0], wm1_ref[...]) + _dot(go_ref[0], wm2_ref[...])
    x1 = _layer_norm(alpha * x + mix, g1_ref[...], b1_ref[...])
    d = x.shape[-1]
    hd = d // XATTN_HEADS
    q = (_dot(x1.astype(BF16), wxq_ref[...]) * (hd ** -0.5)).astype(BF16)
    xa = jnp.zeros_like(x)
    for h in range(XATTN_HEADS):
        cols = slice(h * hd, (h + 1) * hd)
        s = _dot_nt(q[:, cols], km_ref[0, :, cols])
        m = jnp.max(s, axis=-1, keepdims=True)
        p = jnp.exp(s - m)
        l = jnp.sum(p, axis=-1, keepdims=True)
        oh = _dot(p.astype(BF16), vm_ref[0, :, cols]) / l
        xa = xa + _dot(oh.astype(BF16), wxo_ref[cols, :])
    o_ref[0] = _layer_norm(alpha * x1 + xa, g2_ref[...], b2_ref[...])


def _post_mixer(x, moba_o, gla_o, w_mix_o, ln1_g, ln1_b, w_xq, kmem, vmem, w_xo, ln2_g, ln2_b, alpha, tm=512):
    B, S, D = x.shape
    aw = moba_o.shape[-1]
    M = kmem.shape[1]
    wm1 = w_mix_o[:aw].astype(BF16)
    wm2 = w_mix_o[aw:].astype(BF16)
    wxq = w_xq.astype(BF16)
    wxo = w_xo.astype(BF16)
    vec = lambda a: a.reshape(1, D)
    row = lambda w: pl.BlockSpec((1, tm, w), lambda b, s: (b, s, 0))
    full = lambda a: pl.BlockSpec(a.shape, lambda b, s: (0,) * a.ndim)
    vspec = pl.BlockSpec((1, D), lambda b, s: (0, 0))
    memspec = pl.BlockSpec((1, M, D), lambda b, s: (b, 0, 0))
    return pl.pallas_call(
        functools.partial(_post_mixer_kernel, alpha),
        grid=(B, S // tm),
        in_specs=[row(D), row(aw), row(gla_o.shape[-1]), full(wm1), full(wm2), vspec, vspec, full(wxq),
                  memspec, memspec, full(wxo), vspec, vspec],
        out_specs=row(D),
        out_shape=jax.ShapeDtypeStruct((B, S, D), F32),
        compiler_params=pltpu.CompilerParams(
            dimension_semantics=("arbitrary", "arbitrary"), vmem_limit_bytes=VMEM_LIMIT_BYTES),
        name="post_mixer",
    )(x, moba_o, gla_o, wm1, wm2, vec(ln1_g), vec(ln1_b), wxq, kmem, vmem, wxo, vec(ln2_g), vec(ln2_b))


def _mlp_kernel(alpha, fc, x_ref, w1_ref, w2_ref, g_ref, b_ref, o_ref):
    x = x_ref[0]
    xb = x.astype(BF16)
    acc = jnp.zeros_like(x)
    for c in range(w1_ref.shape[1] // fc):
        cols = slice(c * fc, (c + 1) * fc)
        h = jnp.maximum(_dot(xb, w1_ref[:, cols]), 0.0)
        acc = acc + _dot((h * h).astype(BF16), w2_ref[cols, :])
    o_ref[0] = _layer_norm(alpha * x + acc, g_ref[...], b_ref[...])


def _mlp(x, w_ff1, w_ff2, g, b, alpha, tm=512, fc=512):
    B, S, D = x.shape
    w1 = w_ff1.astype(BF16)
    w2 = w_ff2.astype(BF16)
    row = pl.BlockSpec((1, tm, D), lambda bb, s: (bb, s, 0))
    full = lambda a: pl.BlockSpec(a.shape, lambda bb, s: (0, 0))
    vspec = pl.BlockSpec((1, D), lambda bb, s: (0, 0))
    return pl.pallas_call(
        functools.partial(_mlp_kernel, alpha, fc),
        grid=(B, S // tm),
        in_specs=[row, full(w1), full(w2), vspec, vspec],
        out_specs=row,
        out_shape=jax.ShapeDtypeStruct((B, S, D), F32),
        compiler_params=pltpu.CompilerParams(
            dimension_semantics=("arbitrary", "arbitrary"), vmem_limit_bytes=VMEM_LIMIT_BYTES),
        name="sq_relu_mlp",
    )(x, w1, w2, g.reshape(1, D), b.reshape(1, D))


def kernel(x, mem, w_in, w_gate_up, b_gate, gla_norm_g, w_mix_o, ln1_g, ln1_b, w_xq, w_xkv, w_xo,
           ln2_g, ln2_b, w_ff1, w_ff2, ln3_g, ln3_b):
    depth = w_in.shape[0]
    alpha = (2.0 * depth) ** 0.25
    n_heads = w_in.shape[1] // 128
    slopes = 2.0 ** (-8.0 * jnp.arange(1, n_heads + 1, dtype=F32) / n_heads)
    for l in range(depth):
        qa, ka, vat, qg, kg, vg, rg, la = _in_projection(x, w_in[l], w_gate_up[l], b_gate[l])
        moba_o = _moba(qa, ka, vat, slopes)
        gla_o = _gla(qg, kg, la, vg, rg, gla_norm_g[l].reshape(1, GLA_VAL_DIM))
        kmem, vmem = _mem_kv(mem, w_xkv[l])
        x = _post_mixer(x, moba_o, gla_o, w_mix_o[l], ln1_g[l], ln1_b[l], w_xq[l], kmem, vmem, w_xo[l],
                        ln2_g[l], ln2_b[l], alpha)
        x = _mlp(x, w_ff1[l], w_ff2[l], ln3_g[l], ln3_b[l], alpha)
    return x
```

```python
import functools

import jax
import jax.numpy as jnp
from jax import lax
from jax.experimental import pallas as pl
from jax.experimental.pallas import tpu as pltpu

F32 = jnp.float32
BF16 = jnp.bfloat16

MOBA_HEAD_DIM = 64
MOBA_BLOCK = 256
MOBA_TOPK = 3
GLA_HEADS = 4
GLA_KEY_DIM = 64
GLA_VAL_DIM = 128
GLA_GATE_RANK = 16
GLA_GATE_TEMP = 16.0
GLA_CHUNK = 64
XATTN_HEADS = 4
LN_EPS = 1e-5
RMS_EPS = 1e-6

LANES = 128
VMEM_LIMIT_BYTES = 56 * 1024 * 1024

NEG_INF = float("-inf")
LOG2E = 1.4426950408889634


def _dot(a, b):
    return jnp.dot(a, b, preferred_element_type=F32)


def _dot_nt(a, b):
    return lax.dot_general(a, b, (((1,), (1,)), ((), ())), preferred_element_type=F32)


def _dot_tn(a, b):
    return lax.dot_general(a, b, (((0,), (0,)), ((), ())), preferred_element_type=F32)


def _split3(a):
    hi = a.astype(BF16)
    r1 = a - hi.astype(F32)
    mid = r1.astype(BF16)
    lo = (r1 - mid.astype(F32)).astype(BF16)
    return hi, mid, lo


def _layer_norm(x, g, b):
    mu = jnp.mean(x, axis=-1, keepdims=True)
    xc = x - mu
    var = jnp.mean(xc * xc, axis=-1, keepdims=True)
    return xc * lax.rsqrt(var + LN_EPS) * g + b


def _log_sigmoid(z):
    return jnp.minimum(z, 0.0) - jnp.log(1.0 + jnp.exp(-jnp.abs(z)))


def _sigmoid(z):
    return 1.0 / (1.0 + jnp.exp(-z))


def _inproj_kernel(x_ref, wqk_ref, wvat_ref, wg_ref, wvr_ref, wglr_ref, wgu_ref, bg_ref,
                   qa_ref, ka_ref, vat_ref, qg_ref, kg_ref, vg_ref, rg_ref, la_ref):
    xb = x_ref[0].astype(BF16)
    aw = qa_ref.shape[-1]
    qk = _dot(xb, wqk_ref[...])
    qa_ref[0] = (qk[:, :aw] * (MOBA_HEAD_DIM ** -0.5 * LOG2E)).astype(BF16)
    ka_ref[0] = qk[:, aw:].astype(BF16)
    vat_ref[0] = _dot_nt(wvat_ref[...], xb).astype(BF16)
    kw = qg_ref.shape[-1]
    g = _dot(xb, wg_ref[...])
    qg_ref[0] = g[:, :kw] * (GLA_KEY_DIM ** -0.5)
    kg_ref[0] = g[:, kw:]
    vw = vg_ref.shape[-1]
    vr = _dot(xb, wvr_ref[...])
    vg_ref[0] = vr[:, :vw].astype(BF16)
    rg_ref[0] = vr[:, vw:]
    glr = _dot(xb, wglr_ref[...])
    gh, gm, gl = _split3(glr)
    wh, wm, wl = _split3(wgu_ref[...])
    z = (_dot(gh, wh) + _dot(gm, wh) + _dot(gh, wm)
         + _dot(gl, wh) + _dot(gm, wm) + _dot(gh, wl)) + bg_ref[...]
    la_ref[0] = _log_sigmoid(z) / GLA_GATE_TEMP


def _in_projection(x, w_in, w_gate_up, b_gate, tm=512):
    B, S, D = x.shape
    aw = 512
    kw = GLA_HEADS * GLA_KEY_DIM
    vw = GLA_HEADS * GLA_VAL_DIM
    c = 0
    wq = w_in[:, c:c + aw]; c += aw
    wk = w_in[:, c:c + aw]; c += aw
    wv = w_in[:, c:c + aw]; c += aw
    wqg = w_in[:, c:c + kw]; c += kw
    wkg = w_in[:, c:c + kw]; c += kw
    wvg = w_in[:, c:c + vw]; c += vw
    wrg = w_in[:, c:c + vw]; c += vw
    wglr = w_in[:, c:c + GLA_GATE_RANK]
    wqk = jnp.concatenate([wq, wk], axis=1).astype(BF16)
    wvat = wv.T.astype(BF16)
    wg = jnp.concatenate([wqg, wkg], axis=1).astype(BF16)
    wvr = jnp.concatenate([wvg, wrg], axis=1).astype(BF16)
    wglr_p = jnp.pad(wglr, ((0, 0), (0, LANES - GLA_GATE_RANK))).astype(BF16)
    wgu_p = jnp.pad(w_gate_up, ((0, LANES - GLA_GATE_RANK), (0, 0)))
    bg = b_gate.reshape(1, kw)

    def full(a):
        return pl.BlockSpec(a.shape, lambda b, s: (0,) * a.ndim)

    row = lambda w: pl.BlockSpec((1, tm, w), lambda b, s: (b, s, 0))
    out_shape = (
        jax.ShapeDtypeStruct((B, S, aw), BF16),
        jax.ShapeDtypeStruct((B, S, aw), BF16),
        jax.ShapeDtypeStruct((B, aw, S), BF16),
        jax.ShapeDtypeStruct((B, S, kw), F32),
        jax.ShapeDtypeStruct((B, S, kw), F32),
        jax.ShapeDtypeStruct((B, S, vw), BF16),
        jax.ShapeDtypeStruct((B, S, vw), F32),
        jax.ShapeDtypeStruct((B, S, kw), F32),
    )
    out_specs = (row(aw), row(aw), pl.BlockSpec((1, aw, tm), lambda b, s: (b, 0, s)),
                 row(kw), row(kw), row(vw), row(vw), row(kw))
    return pl.pallas_call(
        _inproj_kernel,
        grid=(B, S // tm),
        in_specs=[row(D), full(wqk), full(wvat), full(wg), full(wvr), full(wglr_p), full(wgu_p), full(bg)],
        out_specs=out_specs,
        out_shape=out_shape,
        compiler_params=pltpu.CompilerParams(
            dimension_semantics=("arbitrary", "arbitrary"), vmem_limit_bytes=VMEM_LIMIT_BYTES),
        name="in_projection",
    )(x, wqk, wvat, wg, wvr, wglr_p, wgu_p, bg)


def _alibi_tables(n_heads, bs):
    slopes = 2.0 ** (-8.0 * jnp.arange(1, n_heads + 1, dtype=F32) / n_heads)
    sig = slopes * LOG2E
    rel = jnp.arange(bs, dtype=F32)
    a = -sig[:, None] * rel[None, :]
    sig_b = jnp.broadcast_to(sig[:, None], a.shape)
    cols = list(_split3(a)) + list(_split3(sig_b))
    qaux = jnp.stack(cols + [jnp.zeros_like(cols[0])] * (LANES - len(cols)), axis=-1)
    one = jnp.ones((bs,), F32)
    kcols = [one, one, one, rel, rel, rel] + [jnp.zeros((bs,), F32)] * (LANES - 6)
    kaux = jnp.stack(kcols, axis=-1).astype(BF16)
    return sig, qaux, kaux


def _moba_kernel(sig_ref, q_ref, k_ref, vt_ref, qaux_ref, kaux_ref, o_ref,
                 kmean_ref, sel_ref, acc_ref, m_ref, l_ref):
    hg = pl.program_id(1)
    i = pl.program_id(2)
    bs = MOBA_BLOCK
    nb = k_ref.shape[1] // bs
    hd = MOBA_HEAD_DIM
    n_heads = q_ref.shape[-1] // hd

    @pl.when(i == 0)
    def _():
        for n in range(nb):
            kb = k_ref[0, n * bs:(n + 1) * bs, :].astype(F32)
            kmean_ref[n:n + 1, :] = jnp.sum(kb, axis=0, keepdims=True) * (1.0 / bs)

    lane = lax.broadcasted_iota(jnp.int32, (1, LANES), 1)
    head_lanes = (lane < hd, lane >= hd)
    pair_lanes = [slice((h // 2) * LANES, (h // 2 + 1) * LANES) for h in range(n_heads)]
    qm = []
    for h in range(n_heads):
        qp = q_ref[0, :, pair_lanes[h]]
        qm.append(jnp.where(head_lanes[h % 2], qp, jnp.zeros_like(qp)))
    qaug = [jnp.concatenate([qm[h], qaux_ref[h]], axis=1) for h in range(n_heads)]
    sig = [sig_ref[hg * n_heads + h] for h in range(n_heads)]
    kaux = kaux_ref[...]

    km_pieces = _split3(kmean_ref[...])
    nidx = lax.broadcasted_iota(jnp.int32, (nb, bs), 0)
    for h in range(n_heads):
        g = sum(_dot_nt(kp[:, pair_lanes[h]], qm[h]) for kp in km_pieces)
        g = jnp.where(nidx < i, g, NEG_INF)
        rank = jnp.zeros((nb, bs), jnp.int32)
        for m in range(nb):
            gm = g[m:m + 1, :]
            beats = (gm > g) | ((gm == g) & (m < nidx))
            rank = rank + beats.astype(jnp.int32)
        sel = (rank < MOBA_TOPK) & (nidx < i)
        sel_ref[h] = jnp.where(sel, 0.0, NEG_INF)

    def scores(j):
        start = pl.multiple_of(j * bs, bs)
        kj = k_ref[0, pl.ds(start, bs), :]
        kaug = [jnp.concatenate([kj[:, pair_lanes[2 * p]], kaux], axis=1) for p in range(n_heads // 2)]
        return [_dot_nt(kaug[h // 2], qaug[h]) for h in range(n_heads)]

    def values(j, h):
        start = pl.multiple_of(j * bs, bs)
        return vt_ref[0, h * hd:(h + 1) * hd, pl.ds(start, bs)]

    krow = lax.broadcasted_iota(jnp.int32, (bs, bs), 0)
    qcol = lax.broadcasted_iota(jnp.int32, (bs, bs), 1)
    causal = qcol >= krow
    ts = scores(i)
    for h in range(n_heads):
        t = jnp.where(causal, ts[h], NEG_INF)
        m = jnp.max(t, axis=0, keepdims=True)
        p = jnp.exp2(t - m)
        m_ref[h:h + 1, :] = m
        l_ref[h:h + 1, :] = jnp.sum(p, axis=0, keepdims=True)
        acc_ref[h] = _dot(values(i, h), p.astype(BF16))

    def past_block(j, carry):
        ts = scores(j)
        off = ((i - j) * bs).astype(F32)
        for h in range(n_heads):
            t = ts[h]
            mrow = jnp.max(t, axis=0, keepdims=True)
            cs = sel_ref[h, pl.ds(j, 1), :] - sig[h] * off
            m_old = m_ref[h:h + 1, :]
            m_new = jnp.maximum(m_old, mrow + cs)
            alpha = jnp.exp2(m_old - m_new)
            p = jnp.exp2(t + (cs - m_new))
            l_ref[h:h + 1, :] = alpha * l_ref[h:h + 1, :] + jnp.sum(p, axis=0, keepdims=True)
            m_ref[h:h + 1, :] = m_new
            acc_ref[h] = alpha * acc_ref[h] + _dot(values(j, h), p.astype(BF16))
        return carry

    lax.fori_loop(0, i, past_block, 0)

    for p in range(n_heads // 2):
        ot = jnp.concatenate([acc_ref[2 * p + h] / l_ref[2 * p + h:2 * p + h + 1, :] for h in range(2)], axis=0)
        o_ref[0, :, pair_lanes[2 * p]] = ot.T.astype(BF16)


def _moba(qa, ka, vat, heads_per_step=4):
    B, S, W = qa.shape
    bs = MOBA_BLOCK
    nb = S // bs
    n_heads = W // MOBA_HEAD_DIM
    gw = heads_per_step * MOBA_HEAD_DIM
    sig, qaux, kaux = _alibi_tables(n_heads, bs)
    return pl.pallas_call(
        _moba_kernel,
        grid=(B, W // gw, nb),
        in_specs=[
            pl.BlockSpec(memory_space=pltpu.SMEM),
            pl.BlockSpec((1, bs, gw), lambda b, hg, i: (b, i, hg)),
            pl.BlockSpec((1, S, gw), lambda b, hg, i: (b, 0, hg)),
            pl.BlockSpec((1, gw, S), lambda b, hg, i: (b, hg, 0)),
            pl.BlockSpec((heads_per_step, bs, LANES), lambda b, hg, i: (hg, 0, 0)),
            pl.BlockSpec((bs, LANES), lambda b, hg, i: (0, 0)),
        ],
        out_specs=pl.BlockSpec((1, bs, gw), lambda b, hg, i: (b, i, hg)),
        out_shape=jax.ShapeDtypeStruct((B, S, W), BF16),
        scratch_shapes=[
            pltpu.VMEM((nb, gw), F32),
            pltpu.VMEM((heads_per_step, nb, bs), F32),
            pltpu.VMEM((heads_per_step, MOBA_HEAD_DIM, bs), F32),
            pltpu.VMEM((8, bs), F32),
            pltpu.VMEM((8, bs), F32),
        ],
        compiler_params=pltpu.CompilerParams(
            dimension_semantics=("arbitrary", "arbitrary", "arbitrary"), vmem_limit_bytes=VMEM_LIMIT_BYTES),
        name="moba_attention",
    )(sig, qa, ka, vat, qaux, kaux)


def _gla_kernel(q_ref, k_ref, la_ref, v_ref, r_ref, gn_ref, o_ref, st_ref):
    s_idx = pl.program_id(1)
    C = GLA_CHUNK
    tc = q_ref.shape[1]

    @pl.when(s_idx == 0)
    def _():
        st_ref[...] = jnp.zeros_like(st_ref)

    lane = lax.broadcasted_iota(jnp.int32, (1, LANES), 1)
    head_lanes = (lane < GLA_KEY_DIM, lane >= GLA_KEY_DIM)
    ti = lax.broadcasted_iota(jnp.int32, (C, C), 0)
    si = lax.broadcasted_iota(jnp.int32, (C, C), 1)
    causal = si <= ti
    tril = jnp.where(causal, 1.0, 0.0).astype(BF16)
    gn = gn_ref[...]

    def chunk(c, carry):
        rows = pl.ds(pl.multiple_of(c * C, C), C)
        for hp in range(GLA_HEADS // 2):
            lanes = slice(hp * LANES, (hp + 1) * LANES)
            g = la_ref[0, rows, lanes]
            g_hi, g_mid, g_lo = _split3(g)
            b = _dot(tril, g_hi) + _dot(tril, g_mid) + _dot(tril, g_lo)
            qs = q_ref[0, rows, lanes]
            ks = k_ref[0, rows, lanes]
            b_last = b[C - 1:C, :]
            q_dec = qs * jnp.exp(b)
            k_inv = (ks * jnp.exp(-b)).astype(BF16)
            k_dec = ks * jnp.exp(b_last - b)
            st = st_ref[hp]
            st_b = st.astype(BF16)
            new_st = st * jnp.exp(b_last)
            for h in range(2):
                head = hp * 2 + h
                vl = slice(head * GLA_VAL_DIM, (head + 1) * GLA_VAL_DIM)
                qm = jnp.where(head_lanes[h], q_dec, 0.0).astype(BF16)
                o_inter = _dot_nt(qm, st_b)
                attn = jnp.where(causal, _dot_nt(qm, k_inv), 0.0)
                vh = v_ref[0, rows, vl]
                o = o_inter + _dot(attn.astype(BF16), vh)
                ms = jnp.mean(o * o, axis=-1, keepdims=True)
                on = o * lax.rsqrt(ms + RMS_EPS) * gn
                rr = r_ref[0, rows, vl]
                o_ref[0, rows, vl] = (on * (rr * _sigmoid(rr))).astype(BF16)
                km = jnp.where(head_lanes[h], k_dec, 0.0).astype(BF16)
                new_st = new_st + _dot_tn(vh, km)
            st_ref[hp] = new_st
        return carry

    lax.fori_loop(0, tc // C, chunk, 0)


def _gla(qg, kg, la, vg, rg, gnorm, tc=512):
    B, S, kw = qg.shape
    vw = vg.shape[-1]
    row = lambda w: pl.BlockSpec((1, tc, w), lambda b, s: (b, s, 0))
    return pl.pallas_call(
        _gla_kernel,
        grid=(B, S // tc),
        in_specs=[row(kw), row(kw), row(kw), row(vw), row(vw),
                  pl.BlockSpec((1, GLA_VAL_DIM), lambda b, s: (0, 0))],
        out_specs=row(vw),
        out_shape=jax.ShapeDtypeStruct((B, S, vw), BF16),
        scratch_shapes=[pltpu.VMEM((GLA_HEADS // 2, GLA_VAL_DIM, LANES), F32)],
        compiler_params=pltpu.CompilerParams(
            dimension_semantics=("arbitrary", "arbitrary"), vmem_limit_bytes=VMEM_LIMIT_BYTES),
        name="gla_scan",
    )(qg, kg, la, vg, rg, gnorm)


def _memkv_kernel(mem_ref, w_ref, k_ref, v_ref):
    kv = _dot(mem_ref[0].astype(BF16), w_ref[...])
    d = k_ref.shape[-1]
    k_ref[0] = kv[:, :d].astype(BF16)
    v_ref[0] = kv[:, d:].astype(BF16)


def _mem_kv(mem, w_xkv):
    B, M, D = mem.shape
    wb = w_xkv.astype(BF16)
    blk = pl.BlockSpec((1, M, D), lambda b: (b, 0, 0))
    return pl.pallas_call(
        _memkv_kernel,
        grid=(B,),
        in_specs=[blk, pl.BlockSpec(wb.shape, lambda b: (0, 0))],
        out_specs=(blk, blk),
        out_shape=(jax.ShapeDtypeStruct((B, M, D), BF16), jax.ShapeDtypeStruct((B, M, D), BF16)),
        compiler_params=pltpu.CompilerParams(
            dimension_semantics=("arbitrary",), vmem_limit_bytes=VMEM_LIMIT_BYTES),
        name="mem_kv_projection",
    )(mem, wb)


def _post_mixer_kernel(alpha, x_ref, mo_ref, go_ref, wm1_ref, wm2_ref, g1_ref, b1_ref, wxq_ref,
                       km_ref, vm_ref, wxo_ref, g2_ref, b2_ref, o_ref):
    x = x_ref[0]
    mix = _dot(mo_ref[0], wm1_ref[...]) + _dot(go_ref[0], wm2_ref[...])
    x1 = _layer_norm(alpha * x + mix, g1_ref[...], b1_ref[...])
    d = x.shape[-1]
    hd = d // XATTN_HEADS
    q = (_dot(x1.astype(BF16), wxq_ref[...]) * (hd ** -0.5)).astype(BF16)
    xa = jnp.zeros_like(x)
    for h in range(XATTN_HEADS):
        cols = slice(h * hd, (h + 1) * hd)
        s = _dot_nt(q[:, cols], km_ref[0, :, cols])
        m = jnp.max(s, axis=-1, keepdims=True)
        p = jnp.exp(s - m)
        l = jnp.sum(p, axis=-1, keepdims=True)
        oh = _dot(p.astype(BF16), vm_ref[0, :, cols]) / l
        xa = xa + _dot(oh.astype(BF16), wxo_ref[cols, :])
    o_ref[0] = _layer_norm(alpha * x1 + xa, g2_ref[...], b2_ref[...])


def _post_mixer(x, moba_o, gla_o, w_mix_o, ln1_g, ln1_b, w_xq, kmem, vmem, w_xo, ln2_g, ln2_b, alpha, tm=512):
    B, S, D = x.shape
    aw = moba_o.shape[-1]
    M = kmem.shape[1]
    wm1 = w_mix_o[:aw].astype(BF16)
    wm2 = w_mix_o[aw:].astype(BF16)
    wxq = w_xq.astype(BF16)
    wxo = w_xo.astype(BF16)
    vec = lambda a: a.reshape(1, D)
    row = lambda w: pl.BlockSpec((1, tm, w), lambda b, s: (b, s, 0))
    full = lambda a: pl.BlockSpec(a.shape, lambda b, s: (0,) * a.ndim)
    vspec = pl.BlockSpec((1, D), lambda b, s: (0, 0))
    memspec = pl.BlockSpec((1, M, D), lambda b, s: (b, 0, 0))
    return pl.pallas_call(
        functools.partial(_post_mixer_kernel, alpha),
        grid=(B, S // tm),
        in_specs=[row(D), row(aw), row(gla_o.shape[-1]), full(wm1), full(wm2), vspec, vspec, full(wxq),
                  memspec, memspec, full(wxo), vspec, vspec],
        out_specs=row(D),
        out_shape=jax.ShapeDtypeStruct((B, S, D), F32),
        compiler_params=pltpu.CompilerParams(
            dimension_semantics=("arbitrary", "arbitrary"), vmem_limit_bytes=VMEM_LIMIT_BYTES),
        name="post_mixer",
    )(x, moba_o, gla_o, wm1, wm2, vec(ln1_g), vec(ln1_b), wxq, kmem, vmem, wxo, vec(ln2_g), vec(ln2_b))


def _mlp_kernel(alpha, fc, x_ref, w1_ref, w2_ref, g_ref, b_ref, o_ref):
    x = x_ref[0]
    xb = x.astype(BF16)
    acc = jnp.zeros_like(x)
    for c in range(w1_ref.shape[1] // fc):
        cols = slice(c * fc, (c + 1) * fc)
        h = jnp.maximum(_dot(xb, w1_ref[:, cols]), 0.0)
        acc = acc + _dot((h * h).astype(BF16), w2_ref[cols, :])
    o_ref[0] = _layer_norm(alpha * x + acc, g_ref[...], b_ref[...])


def _mlp(x, w_ff1, w_ff2, g, b, alpha, tm=512, fc=512):
    B, S, D = x.shape
    w1 = w_ff1.astype(BF16)
    w2 = w_ff2.astype(BF16)
    row = pl.BlockSpec((1, tm, D), lambda bb, s: (bb, s, 0))
    full = lambda a: pl.BlockSpec(a.shape, lambda bb, s: (0, 0))
    vspec = pl.BlockSpec((1, D), lambda bb, s: (0, 0))
    return pl.pallas_call(
        functools.partial(_mlp_kernel, alpha, fc),
        grid=(B, S // tm),
        in_specs=[row, full(w1), full(w2), vspec, vspec],
        out_specs=row,
        out_shape=jax.ShapeDtypeStruct((B, S, D), F32),
        compiler_params=pltpu.CompilerParams(
            dimension_semantics=("arbitrary", "arbitrary"), vmem_limit_bytes=VMEM_LIMIT_BYTES),
        name="sq_relu_mlp",
    )(x, w1, w2, g.reshape(1, D), b.reshape(1, D))


def kernel(x, mem, w_in, w_gate_up, b_gate, gla_norm_g, w_mix_o, ln1_g, ln1_b, w_xq, w_xkv, w_xo,
           ln2_g, ln2_b, w_ff1, w_ff2, ln3_g, ln3_b):
    depth = w_in.shape[0]
    alpha = (2.0 * depth) ** 0.25
    for l in range(depth):
        qa, ka, vat, qg, kg, vg, rg, la = _in_projection(x, w_in[l], w_gate_up[l], b_gate[l])
        moba_o = _moba(qa, ka, vat)
        gla_o = _gla(qg, kg, la, vg, rg, gla_norm_g[l].reshape(1, GLA_VAL_DIM))
        kmem, vmem = _mem_kv(mem, w_xkv[l])
        x = _post_mixer(x, moba_o, gla_o, w_mix_o[l], ln1_g[l], ln1_b[l], w_xq[l], kmem, vmem, w_xo[l],
                        ln2_g[l], ln2_b[l], alpha)
        x = _mlp(x, w_ff1[l], w_ff2[l], ln3_g[l], ln3_b[l], alpha)
    return x
```

```python
import functools

import jax
import jax.numpy as jnp
from jax import lax
from jax.experimental import pallas as pl
from jax.experimental.pallas import tpu as pltpu

F32 = jnp.float32
BF16 = jnp.bfloat16

MOBA_HEAD_DIM = 64
MOBA_VALUE_ROWS = 80
MOBA_BLOCK = 256
MOBA_TOPK = 3
GLA_HEADS = 4
GLA_KEY_DIM = 64
GLA_VAL_DIM = 128
GLA_GATE_RANK = 16
GLA_GATE_TEMP = 16.0
GLA_CHUNK = 64
XATTN_HEADS = 4
LN_EPS = 1e-5
RMS_EPS = 1e-6

LANES = 128
VMEM_LIMIT_BYTES = 56 * 1024 * 1024

NEG_INF = float("-inf")
LOG2E = 1.4426950408889634


def _dot(a, b):
    return jnp.dot(a, b, preferred_element_type=F32)


def _dot_nt(a, b):
    return lax.dot_general(a, b, (((1,), (1,)), ((), ())), preferred_element_type=F32)


def _dot_tn(a, b):
    return lax.dot_general(a, b, (((0,), (0,)), ((), ())), preferred_element_type=F32)


def _split3(a):
    hi = a.astype(BF16)
    r1 = a - hi.astype(F32)
    mid = r1.astype(BF16)
    lo = (r1 - mid.astype(F32)).astype(BF16)
    return hi, mid, lo


def _layer_norm(x, g, b):
    mu = jnp.mean(x, axis=-1, keepdims=True)
    xc = x - mu
    var = jnp.mean(xc * xc, axis=-1, keepdims=True)
    return xc * lax.rsqrt(var + LN_EPS) * g + b


def _log_sigmoid(z):
    return jnp.minimum(z, 0.0) - jnp.log(1.0 + jnp.exp(-jnp.abs(z)))


def _sigmoid(z):
    return 1.0 / (1.0 + jnp.exp(-z))


def _inproj_kernel(x_ref, wqk_ref, wvat_ref, wg_ref, wvr_ref, wglr_ref, wgu_ref, bg_ref,
                   qa_ref, ka_ref, vat_ref, qg_ref, kg_ref, vg_ref, rg_ref, la_ref):
    xb = x_ref[0].astype(BF16)
    aw = qa_ref.shape[-1]
    qk = _dot(xb, wqk_ref[...])
    qa_ref[0] = (qk[:, :aw] * (MOBA_HEAD_DIM ** -0.5 * LOG2E)).astype(BF16)
    ka_ref[0] = qk[:, aw:].astype(BF16)
    vt = _dot_nt(wvat_ref[...], xb).astype(BF16)
    hd, va = MOBA_HEAD_DIM, MOBA_VALUE_ROWS
    for h in range(aw // hd):
        vat_ref[0, h * va:h * va + hd, :] = vt[h * hd:(h + 1) * hd, :]
        vat_ref[0, h * va + hd:(h + 1) * va, :] = jnp.ones((va - hd, vt.shape[1]), BF16)
    kw = qg_ref.shape[-1]
    g = _dot(xb, wg_ref[...])
    qg_ref[0] = g[:, :kw] * (GLA_KEY_DIM ** -0.5)
    kg_ref[0] = g[:, kw:]
    vw = vg_ref.shape[-1]
    vr = _dot(xb, wvr_ref[...])
    vg_ref[0] = vr[:, :vw].astype(BF16)
    rg_ref[0] = vr[:, vw:]
    glr = _dot(xb, wglr_ref[...])
    gh, gm, gl = _split3(glr)
    wh, wm, wl = _split3(wgu_ref[...])
    z = (_dot(gh, wh) + _dot(gm, wh) + _dot(gh, wm)
         + _dot(gl, wh) + _dot(gm, wm) + _dot(gh, wl)) + bg_ref[...]
    la_ref[0] = _log_sigmoid(z) / GLA_GATE_TEMP


def _in_projection(x, w_in, w_gate_up, b_gate, tm=512):
    B, S, D = x.shape
    aw = 512
    vrows = aw // MOBA_HEAD_DIM * MOBA_VALUE_ROWS
    kw = GLA_HEADS * GLA_KEY_DIM
    vw = GLA_HEADS * GLA_VAL_DIM
    c = 0
    wq = w_in[:, c:c + aw]; c += aw
    wk = w_in[:, c:c + aw]; c += aw
    wv = w_in[:, c:c + aw]; c += aw
    wqg = w_in[:, c:c + kw]; c += kw
    wkg = w_in[:, c:c + kw]; c += kw
    wvg = w_in[:, c:c + vw]; c += vw
    wrg = w_in[:, c:c + vw]; c += vw
    wglr = w_in[:, c:c + GLA_GATE_RANK]
    wqk = jnp.concatenate([wq, wk], axis=1).astype(BF16)
    wvat = wv.T.astype(BF16)
    wg = jnp.concatenate([wqg, wkg], axis=1).astype(BF16)
    wvr = jnp.concatenate([wvg, wrg], axis=1).astype(BF16)
    wglr_p = jnp.pad(wglr, ((0, 0), (0, LANES - GLA_GATE_RANK))).astype(BF16)
    wgu_p = jnp.pad(w_gate_up, ((0, LANES - GLA_GATE_RANK), (0, 0)))
    bg = b_gate.reshape(1, kw)

    def full(a):
        return pl.BlockSpec(a.shape, lambda b, s: (0,) * a.ndim)

    row = lambda w: pl.BlockSpec((1, tm, w), lambda b, s: (b, s, 0))
    out_shape = (
        jax.ShapeDtypeStruct((B, S, aw), BF16),
        jax.ShapeDtypeStruct((B, S, aw), BF16),
        jax.ShapeDtypeStruct((B, vrows, S), BF16),
        jax.ShapeDtypeStruct((B, S, kw), F32),
        jax.ShapeDtypeStruct((B, S, kw), F32),
        jax.ShapeDtypeStruct((B, S, vw), BF16),
        jax.ShapeDtypeStruct((B, S, vw), F32),
        jax.ShapeDtypeStruct((B, S, kw), F32),
    )
    out_specs = (row(aw), row(aw), pl.BlockSpec((1, vrows, tm), lambda b, s: (b, 0, s)),
                 row(kw), row(kw), row(vw), row(vw), row(kw))
    return pl.pallas_call(
        _inproj_kernel,
        grid=(B, S // tm),
        in_specs=[row(D), full(wqk), full(wvat), full(wg), full(wvr), full(wglr_p), full(wgu_p), full(bg)],
        out_specs=out_specs,
        out_shape=out_shape,
        compiler_params=pltpu.CompilerParams(
            dimension_semantics=("arbitrary", "arbitrary"), vmem_limit_bytes=VMEM_LIMIT_BYTES),
        name="in_projection",
    )(x, wqk, wvat, wg, wvr, wglr_p, wgu_p, bg)


def _alibi_tables(n_heads, bs):
    slopes = 2.0 ** (-8.0 * jnp.arange(1, n_heads + 1, dtype=F32) / n_heads)
    sig = slopes * LOG2E
    rel = jnp.arange(bs, dtype=F32)
    a = -sig[:, None] * rel[None, :]
    sig_b = jnp.broadcast_to(sig[:, None], a.shape)
    cols = list(_split3(a)) + list(_split3(sig_b))
    qaux = jnp.stack(cols + [jnp.zeros_like(cols[0])] * (LANES - len(cols)), axis=-1)
    one = jnp.ones((bs,), F32)
    kcols = [one, one, one, rel, rel, rel] + [jnp.zeros((bs,), F32)] * (LANES - 6)
    kaux = jnp.stack(kcols, axis=-1).astype(BF16)
    return sig, qaux, kaux


def _moba_kernel(sig_ref, q_ref, k_ref, vt_ref, qaux_ref, kaux_ref, o_ref,
                 kmean_ref, kms_ref, sel_ref, acc_ref, m_ref, s_ref, mx_ref):
    hg = pl.program_id(1)
    i = pl.program_id(2)
    bs = MOBA_BLOCK
    nb = k_ref.shape[1] // bs
    hd = MOBA_HEAD_DIM
    va = MOBA_VALUE_ROWS
    n_heads = q_ref.shape[-1] // hd

    @pl.when(i == 0)
    def _():
        for n in range(nb):
            kb = k_ref[0, n * bs:(n + 1) * bs, :].astype(F32)
            kmean_ref[n:n + 1, :] = jnp.sum(kb, axis=0, keepdims=True) * (1.0 / bs)
        for r, piece in enumerate(_split3(kmean_ref[...])):
            kms_ref[r * nb:(r + 1) * nb, :] = piece

    lane = lax.broadcasted_iota(jnp.int32, (1, LANES), 1)
    head_lanes = (lane < hd, lane >= hd)
    pair_lanes = [slice((h // 2) * LANES, (h // 2 + 1) * LANES) for h in range(n_heads)]
    qm = []
    for h in range(n_heads):
        qp = q_ref[0, :, pair_lanes[h]]
        qm.append(jnp.where(head_lanes[h % 2], qp, jnp.zeros_like(qp)))
    qaug = [jnp.concatenate([qm[h], qaux_ref[h]], axis=1) for h in range(n_heads)]
    sig = [sig_ref[hg * n_heads + h] for h in range(n_heads)]
    kaux = kaux_ref[...]

    def select_blocks(h, g):
        nidx = lax.broadcasted_iota(jnp.int32, (nb, bs), 0)
        g = jnp.where(nidx < i, g, NEG_INF)
        sel = jnp.zeros((nb, bs), jnp.bool_)
        for _ in range(MOBA_TOPK):
            best = jnp.max(g, axis=0, keepdims=True)
            first = jnp.min(jnp.where(g == best, nidx, nb), axis=0, keepdims=True)
            pick = nidx == first
            sel = sel | pick
            g = jnp.where(pick, NEG_INF, g)
        sel_ref[h] = jnp.where(sel & (nidx < i), 0.0, NEG_INF)

    def stage_a(blk, slot, with_gate=False):
        start = pl.multiple_of(blk * bs, bs)
        kj = k_ref[0, pl.ds(start, bs), :]
        kaug = [jnp.concatenate([kj[:, pair_lanes[2 * p]], kaux], axis=1) for p in range(n_heads // 2)]
        if with_gate:
            zeros = jnp.zeros((3 * nb, LANES), BF16)
            kaug = [jnp.concatenate([kaug[p], jnp.concatenate([kms_ref[:, pair_lanes[2 * p]], zeros], axis=1)], axis=0)
                    for p in range(n_heads // 2)]
        for h in range(n_heads):
            t = _dot_nt(kaug[h // 2], qaug[h])
            if with_gate:
                select_blocks(h, t[bs:bs + nb] + t[bs + nb:bs + 2 * nb] + t[bs + 2 * nb:])
                t = t[:bs]
            s_ref[slot, h] = t
            mx_ref[slot, h:h + 1, :] = jnp.max(t, axis=0, keepdims=True)

    def values(blk, h):
        start = pl.multiple_of(blk * bs, bs)
        return vt_ref[0, h * va:(h + 1) * va, pl.ds(start, bs)]

    def stage_b_own(slot):
        krow = lax.broadcasted_iota(jnp.int32, (bs, bs), 0)
        qcol = lax.broadcasted_iota(jnp.int32, (bs, bs), 1)
        causal = qcol >= krow
        for h in range(n_heads):
            t = jnp.where(causal, s_ref[slot, h], NEG_INF)
            m = jnp.max(t, axis=0, keepdims=True)
            p = jnp.exp2(t - m)
            m_ref[h:h + 1, :] = m
            acc_ref[h] = _dot(values(i, h), p.astype(BF16))

    def stage_b_past(blk, slot):
        off = ((i - blk) * bs).astype(F32)
        for h in range(n_heads):
            cs = sel_ref[h, pl.ds(blk, 1), :] - sig[h] * off
            m_old = m_ref[h:h + 1, :]
            m_new = jnp.maximum(m_old, mx_ref[slot, h:h + 1, :] + cs)
            alpha = jnp.exp2(m_old - m_new)
            p = jnp.exp2(s_ref[slot, h] + (cs - m_new))
            m_ref[h:h + 1, :] = m_new
            acc_ref[h] = alpha * acc_ref[h] + _dot(values(blk, h), p.astype(BF16))

    stage_a(i, 0, with_gate=True)
    stage_a(jnp.maximum(i - 1, 0), 1)
    stage_b_own(0)

    def step(u, carry):
        blk = i - 1 - 2 * u
        stage_a(blk - 1, 0)
        stage_b_past(blk, 1)
        stage_a(jnp.maximum(blk - 2, 0), 1)
        stage_b_past(blk - 1, 0)
        return carry

    lax.fori_loop(0, i // 2, step, 0)

    @pl.when(i % 2 == 1)
    def _():
        stage_b_past(0, 1)

    for p in range(n_heads // 2):
        ot = jnp.concatenate([acc_ref[2 * p + h, :hd, :] / acc_ref[2 * p + h, hd:hd + 1, :] for h in range(2)], axis=0)
        o_ref[0, :, pair_lanes[2 * p]] = ot.T.astype(BF16)


def _moba(qa, ka, vat, heads_per_step=4):
    B, S, W = qa.shape
    bs = MOBA_BLOCK
    nb = S // bs
    n_heads = W // MOBA_HEAD_DIM
    gw = heads_per_step * MOBA_HEAD_DIM
    sig, qaux, kaux = _alibi_tables(n_heads, bs)
    return pl.pallas_call(
        _moba_kernel,
        grid=(B, W // gw, nb),
        in_specs=[
            pl.BlockSpec(memory_space=pltpu.SMEM),
            pl.BlockSpec((1, bs, gw), lambda b, hg, i: (b, i, hg)),
            pl.BlockSpec((1, S, gw), lambda b, hg, i: (b, 0, hg)),
            pl.BlockSpec((1, heads_per_step * MOBA_VALUE_ROWS, S), lambda b, hg, i: (b, hg, 0)),
            pl.BlockSpec((heads_per_step, bs, LANES), lambda b, hg, i: (hg, 0, 0)),
            pl.BlockSpec((bs, LANES), lambda b, hg, i: (0, 0)),
        ],
        out_specs=pl.BlockSpec((1, bs, gw), lambda b, hg, i: (b, i, hg)),
        out_shape=jax.ShapeDtypeStruct((B, S, W), BF16),
        scratch_shapes=[
            pltpu.VMEM((nb, gw), F32),
            pltpu.VMEM((3 * nb, gw), BF16),
            pltpu.VMEM((heads_per_step, nb, bs), F32),
            pltpu.VMEM((heads_per_step, MOBA_VALUE_ROWS, bs), F32),
            pltpu.VMEM((8, bs), F32),
            pltpu.VMEM((2, heads_per_step, bs, bs), F32),
            pltpu.VMEM((2, 8, bs), F32),
        ],
        compiler_params=pltpu.CompilerParams(
            dimension_semantics=("arbitrary", "arbitrary", "arbitrary"), vmem_limit_bytes=VMEM_LIMIT_BYTES),
        name="moba_attention",
    )(sig, qa, ka, vat, qaux, kaux)


def _gla_kernel(q_ref, k_ref, la_ref, v_ref, r_ref, gn_ref, o_ref, st_ref):
    s_idx = pl.program_id(1)
    C = GLA_CHUNK
    tc = q_ref.shape[1]

    @pl.when(s_idx == 0)
    def _():
        st_ref[...] = jnp.zeros_like(st_ref)

    lane = lax.broadcasted_iota(jnp.int32, (1, LANES), 1)
    head_lanes = (lane < GLA_KEY_DIM, lane >= GLA_KEY_DIM)
    ti = lax.broadcasted_iota(jnp.int32, (C, C), 0)
    si = lax.broadcasted_iota(jnp.int32, (C, C), 1)
    causal = si <= ti
    tril = jnp.where(causal, 1.0, 0.0).astype(BF16)
    gn = gn_ref[...]

    def chunk(c, carry):
        rows = pl.ds(pl.multiple_of(c * C, C), C)
        for hp in range(GLA_HEADS // 2):
            lanes = slice(hp * LANES, (hp + 1) * LANES)
            g = la_ref[0, rows, lanes]
            g_hi, g_mid, g_lo = _split3(g)
            b = _dot(tril, g_hi) + _dot(tril, g_mid) + _dot(tril, g_lo)
            qs = q_ref[0, rows, lanes]
            ks = k_ref[0, rows, lanes]
            b_last = b[C - 1:C, :]
            q_dec = qs * jnp.exp(b)
            k_inv = (ks * jnp.exp(-b)).astype(BF16)
            k_dec = ks * jnp.exp(b_last - b)
            st = st_ref[hp]
            st_b = st.astype(BF16)
            new_st = st * jnp.exp(b_last)
            for h in range(2):
                head = hp * 2 + h
                vl = slice(head * GLA_VAL_DIM, (head + 1) * GLA_VAL_DIM)
                qm = jnp.where(head_lanes[h], q_dec, 0.0).astype(BF16)
                o_inter = _dot_nt(qm, st_b)
                attn = jnp.where(causal, _dot_nt(qm, k_inv), 0.0)
                vh = v_ref[0, rows, vl]
                o = o_inter + _dot(attn.astype(BF16), vh)
                ms = jnp.mean(o * o, axis=-1, keepdims=True)
                on = o * lax.rsqrt(ms + RMS_EPS) * gn
                rr = r_ref[0, rows, vl]
                o_ref[0, rows, vl] = (on * (rr * _sigmoid(rr))).astype(BF16)
                km = jnp.where(head_lanes[h], k_dec, 0.0).astype(BF16)
                new_st = new_st + _dot_tn(vh, km)
            st_ref[hp] = new_st
        return carry

    lax.fori_loop(0, tc // C, chunk, 0)


def _gla(qg, kg, la, vg, rg, gnorm, tc=512):
    B, S, kw = qg.shape
    vw = vg.shape[-1]
    row = lambda w: pl.BlockSpec((1, tc, w), lambda b, s: (b, s, 0))
    return pl.pallas_call(
        _gla_kernel,
        grid=(B, S // tc),
        in_specs=[row(kw), row(kw), row(kw), row(vw), row(vw),
                  pl.BlockSpec((1, GLA_VAL_DIM), lambda b, s: (0, 0))],
        out_specs=row(vw),
        out_shape=jax.ShapeDtypeStruct((B, S, vw), BF16),
        scratch_shapes=[pltpu.VMEM((GLA_HEADS // 2, GLA_VAL_DIM, LANES), F32)],
        compiler_params=pltpu.CompilerParams(
            dimension_semantics=("arbitrary", "arbitrary"), vmem_limit_bytes=VMEM_LIMIT_BYTES),
        name="gla_scan",
    )(qg, kg, la, vg, rg, gnorm)


def _memkv_kernel(mem_ref, w_ref, k_ref, v_ref):
    kv = _dot(mem_ref[0].astype(BF16), w_ref[...])
    d = k_ref.shape[-1]
    k_ref[0] = kv[:, :d].astype(BF16)
    v_ref[0] = kv[:, d:].astype(BF16)


def _mem_kv(mem, w_xkv):
    B, M, D = mem.shape
    wb = w_xkv.astype(BF16)
    blk = pl.BlockSpec((1, M, D), lambda b: (b, 0, 0))
    return pl.pallas_call(
        _memkv_kernel,
        grid=(B,),
        in_specs=[blk, pl.BlockSpec(wb.shape, lambda b: (0, 0))],
        out_specs=(blk, blk),
        out_shape=(jax.ShapeDtypeStruct((B, M, D), BF16), jax.ShapeDtypeStruct((B, M, D), BF16)),
        compiler_params=pltpu.CompilerParams(
            dimension_semantics=("arbitrary",), vmem_limit_bytes=VMEM_LIMIT_BYTES),
        name="mem_kv_projection",
    )(mem, wb)


def _post_mixer_kernel(alpha, x_ref, mo_ref, go_ref, wm1_ref, wm2_ref, g1_ref, b1_ref, wxq_ref,
                       km_ref, vm_ref, wxo_ref, g2_ref, b2_ref, o_ref):
    x = x_ref[0]
    mix = _dot(mo_ref[0], wm1_ref[...]) + _dot(go_ref[0], wm2_ref[...])
    x1 = _layer_norm(alpha * x + mix, g1_ref[...], b1_ref[...])
    d = x.shape[-1]
    hd = d // XATTN_HEADS
    q = (_dot(x1.astype(BF16), wxq_ref[...]) * (hd ** -0.5)).astype(BF16)
    xa = jnp.zeros_like(x)
    for h in range(XATTN_HEADS):
        cols = slice(h * hd, (h + 1) * hd)
        s = _dot_nt(q[:, cols], km_ref[0, :, cols])
        m = jnp.max(s, axis=-1, keepdims=True)
        p = jnp.exp(s - m)
        l = jnp.sum(p, axis=-1, keepdims=True)
        oh = _dot(p.astype(BF16), vm_ref[0, :, cols]) / l
        xa = xa + _dot(oh.astype(BF16), wxo_ref[cols, :])
    o_ref[0] = _layer_norm(alpha * x1 + xa, g2_ref[...], b2_ref[...])


def _post_mixer(x, moba_o, gla_o, w_mix_o, ln1_g, ln1_b, w_xq, kmem, vmem, w_xo, ln2_g, ln2_b, alpha, tm=512):
    B, S, D = x.shape
    aw = moba_o.shape[-1]
    M = kmem.shape[1]
    wm1 = w_mix_o[:aw].astype(BF16)
    wm2 = w_mix_o[aw:].astype(BF16)
    wxq = w_xq.astype(BF16)
    wxo = w_xo.astype(BF16)
    vec = lambda a: a.reshape(1, D)
    row = lambda w: pl.BlockSpec((1, tm, w), lambda b, s: (b, s, 0))
    full = lambda a: pl.BlockSpec(a.shape, lambda b, s: (0,) * a.ndim)
    vspec = pl.BlockSpec((1, D), lambda b, s: (0, 0))
    memspec = pl.BlockSpec((1, M, D), lambda b, s: (b, 0, 0))
    return pl.pallas_call(
        functools.partial(_post_mixer_kernel, alpha),
        grid=(B, S // tm),
        in_specs=[row(D), row(aw), row(gla_o.shape[-1]), full(wm1), full(wm2), vspec, vspec, full(wxq),
                  memspec, memspec, full(wxo), vspec, vspec],
        out_specs=row(D),
        out_shape=jax.ShapeDtypeStruct((B, S, D), F32),
        compiler_params=pltpu.CompilerParams(
            dimension_semantics=("arbitrary", "arbitrary"), vmem_limit_bytes=VMEM_LIMIT_BYTES),
        name="post_mixer",
    )(x, moba_o, gla_o, wm1, wm2, vec(ln1_g), vec(ln1_b), wxq, kmem, vmem, wxo, vec(ln2_g), vec(ln2_b))


def _mlp_kernel(alpha, fc, x_ref, w1_ref, w2_ref, g_ref, b_ref, o_ref):
    x = x_ref[0]
    xb = x.astype(BF16)
    acc = jnp.zeros_like(x)
    for c in range(w1_ref.shape[1] // fc):
        cols = slice(c * fc, (c + 1) * fc)
        h = jnp.maximum(_dot(xb, w1_ref[:, cols]), 0.0)
        acc = acc + _dot((h * h).astype(BF16), w2_ref[cols, :])
    o_ref[0] = _layer_norm(alpha * x + acc, g_ref[...], b_ref[...])


def _mlp(x, w_ff1, w_ff2, g, b, alpha, tm=512, fc=512):
    B, S, D = x.shape
    w1 = w_ff1.astype(BF16)
    w2 = w_ff2.astype(BF16)
    row = pl.BlockSpec((1, tm, D), lambda bb, s: (bb, s, 0))
    full = lambda a: pl.BlockSpec(a.shape, lambda bb, s: (0, 0))
    vspec = pl.BlockSpec((1, D), lambda bb, s: (0, 0))
    return pl.pallas_call(
        functools.partial(_mlp_kernel, alpha, fc),
        grid=(B, S // tm),
        in_specs=[row, full(w1), full(w2), vspec, vspec],
        out_specs=row,
        out_shape=jax.ShapeDtypeStruct((B, S, D), F32),
        compiler_params=pltpu.CompilerParams(
            dimension_semantics=("arbitrary", "arbitrary"), vmem_limit_bytes=VMEM_LIMIT_BYTES),
        name="sq_relu_mlp",
    )(x, w1, w2, g.reshape(1, D), b.reshape(1, D))


def kernel(x, mem, w_in, w_gate_up, b_gate, gla_norm_g, w_mix_o, ln1_g, ln1_b, w_xq, w_xkv, w_xo,
           ln2_g, ln2_b, w_ff1, w_ff2, ln3_g, ln3_b):
    depth = w_in.shape[0]
    alpha = (2.0 * depth) ** 0.25
    for l in range(depth):
        qa, ka, vat, qg, kg, vg, rg, la = _in_projection(x, w_in[l], w_gate_up[l], b_gate[l])
        moba_o = _moba(qa, ka, vat)
        gla_o = _gla(qg, kg, la, vg, rg, gla_norm_g[l].reshape(1, GLA_VAL_DIM))
        kmem, vmem = _mem_kv(mem, w_xkv[l])
        x = _post_mixer(x, moba_o, gla_o, w_mix_o[l], ln1_g[l], ln1_b[l], w_xq[l], kmem, vmem, w_xo[l],
                        ln2_g[l], ln2_b[l], alpha)
        x = _mlp(x, w_ff1[l], w_ff2[l], ln3_g[l], ln3_b[l], alpha)
    return x
```

```python
import functools

import jax
import jax.numpy as jnp
from jax import lax
from jax.experimental import pallas as pl
from jax.experimental.pallas import tpu as pltpu

F32 = jnp.float32
BF16 = jnp.bfloat16

MOBA_HEAD_DIM = 64
MOBA_VALUE_ROWS = 80
MOBA_BLOCK = 256
MOBA_TOPK = 3
GLA_HEADS = 4
GLA_KEY_DIM = 64
GLA_VAL_DIM = 128
GLA_GATE_RANK = 16
GLA_GATE_TEMP = 16.0
GLA_CHUNK = 64
GLA_GROUP = 256
XATTN_HEADS = 4
LN_EPS = 1e-5
RMS_EPS = 1e-6

LANES = 128
VMEM_LIMIT_BYTES = 56 * 1024 * 1024

NEG_INF = float("-inf")
LOG2E = 1.4426950408889634


def _dot(a, b):
    return jnp.dot(a, b, preferred_element_type=F32)


def _dot_nt(a, b):
    return lax.dot_general(a, b, (((1,), (1,)), ((), ())), preferred_element_type=F32)


def _dot_tn(a, b):
    return lax.dot_general(a, b, (((0,), (0,)), ((), ())), preferred_element_type=F32)


def _split3(a):
    hi = a.astype(BF16)
    r1 = a - hi.astype(F32)
    mid = r1.astype(BF16)
    lo = (r1 - mid.astype(F32)).astype(BF16)
    return hi, mid, lo


def _layer_norm(x, g, b):
    mu = jnp.mean(x, axis=-1, keepdims=True)
    xc = x - mu
    var = jnp.mean(xc * xc, axis=-1, keepdims=True)
    return xc * lax.rsqrt(var + LN_EPS) * g + b


def _log_sigmoid(z):
    return jnp.minimum(z, 0.0) - jnp.log(1.0 + jnp.exp(-jnp.abs(z)))


def _sigmoid(z):
    return 1.0 / (1.0 + jnp.exp(-z))


def _inproj_kernel(x_ref, wqk_ref, wvat_ref, wg_ref, wvr_ref, wglr_ref, wgu_ref, bg_ref,
                   qa_ref, ka_ref, vat_ref, qg_ref, kg_ref, vg_ref, rg_ref, la_ref):
    xb = x_ref[0].astype(BF16)
    aw = qa_ref.shape[-1]
    qk = _dot(xb, wqk_ref[...])
    qa_ref[0] = (qk[:, :aw] * (MOBA_HEAD_DIM ** -0.5 * LOG2E)).astype(BF16)
    ka_ref[0] = qk[:, aw:].astype(BF16)
    vt = _dot_nt(wvat_ref[...], xb).astype(BF16)
    hd, va = MOBA_HEAD_DIM, MOBA_VALUE_ROWS
    for h in range(aw // hd):
        vat_ref[0, h * va:h * va + hd, :] = vt[h * hd:(h + 1) * hd, :]
        vat_ref[0, h * va + hd:(h + 1) * va, :] = jnp.ones((va - hd, vt.shape[1]), BF16)
    kw = qg_ref.shape[-1]
    g = _dot(xb, wg_ref[...])
    qg_ref[0] = g[:, :kw] * (GLA_KEY_DIM ** -0.5)
    kg_ref[0] = g[:, kw:]
    vw = vg_ref.shape[-1]
    vr = _dot(xb, wvr_ref[...])
    vg_ref[0] = vr[:, :vw].astype(BF16)
    rg = vr[:, vw:]
    rg_ref[0] = rg * _sigmoid(rg)
    glr = _dot(xb, wglr_ref[...])
    gh, gm, gl = _split3(glr)
    wh, wm, wl = _split3(wgu_ref[...])
    z = (_dot(gh, wh) + _dot(gm, wh) + _dot(gh, wm)
         + _dot(gl, wh) + _dot(gm, wm) + _dot(gh, wl)) + bg_ref[...]
    la_ref[0] = _log_sigmoid(z) / GLA_GATE_TEMP


def _in_projection(x, w_in, w_gate_up, b_gate, tm=512):
    B, S, D = x.shape
    aw = 512
    vrows = aw // MOBA_HEAD_DIM * MOBA_VALUE_ROWS
    kw = GLA_HEADS * GLA_KEY_DIM
    vw = GLA_HEADS * GLA_VAL_DIM
    c = 0
    wq = w_in[:, c:c + aw]; c += aw
    wk = w_in[:, c:c + aw]; c += aw
    wv = w_in[:, c:c + aw]; c += aw
    wqg = w_in[:, c:c + kw]; c += kw
    wkg = w_in[:, c:c + kw]; c += kw
    wvg = w_in[:, c:c + vw]; c += vw
    wrg = w_in[:, c:c + vw]; c += vw
    wglr = w_in[:, c:c + GLA_GATE_RANK]
    wqk = jnp.concatenate([wq, wk], axis=1).astype(BF16)
    wvat = wv.T.astype(BF16)
    wg = jnp.concatenate([wqg, wkg], axis=1).astype(BF16)
    wvr = jnp.concatenate([wvg, wrg], axis=1).astype(BF16)
    wglr_p = jnp.pad(wglr, ((0, 0), (0, LANES - GLA_GATE_RANK))).astype(BF16)
    wgu_p = jnp.pad(w_gate_up, ((0, LANES - GLA_GATE_RANK), (0, 0)))
    bg = b_gate.reshape(1, kw)

    def full(a):
        return pl.BlockSpec(a.shape, lambda b, s: (0,) * a.ndim)

    row = lambda w: pl.BlockSpec((1, tm, w), lambda b, s: (b, s, 0))
    out_shape = (
        jax.ShapeDtypeStruct((B, S, aw), BF16),
        jax.ShapeDtypeStruct((B, S, aw), BF16),
        jax.ShapeDtypeStruct((B, vrows, S), BF16),
        jax.ShapeDtypeStruct((B, S, kw), F32),
        jax.ShapeDtypeStruct((B, S, kw), F32),
        jax.ShapeDtypeStruct((B, S, vw), BF16),
        jax.ShapeDtypeStruct((B, S, vw), F32),
        jax.ShapeDtypeStruct((B, S, kw), F32),
    )
    out_specs = (row(aw), row(aw), pl.BlockSpec((1, vrows, tm), lambda b, s: (b, 0, s)),
                 row(kw), row(kw), row(vw), row(vw), row(kw))
    return pl.pallas_call(
        _inproj_kernel,
        grid=(B, S // tm),
        in_specs=[row(D), full(wqk), full(wvat), full(wg), full(wvr), full(wglr_p), full(wgu_p), full(bg)],
        out_specs=out_specs,
        out_shape=out_shape,
        compiler_params=pltpu.CompilerParams(
            dimension_semantics=("arbitrary", "arbitrary"), vmem_limit_bytes=VMEM_LIMIT_BYTES),
        name="in_projection",
    )(x, wqk, wvat, wg, wvr, wglr_p, wgu_p, bg)


def _alibi_tables(n_heads, bs):
    slopes = 2.0 ** (-8.0 * jnp.arange(1, n_heads + 1, dtype=F32) / n_heads)
    sig = slopes * LOG2E
    rel = jnp.arange(bs, dtype=F32)
    a = -sig[:, None] * rel[None, :]
    sig_b = jnp.broadcast_to(sig[:, None], a.shape)
    cols = list(_split3(a)) + list(_split3(sig_b))
    qaux = jnp.stack(cols + [jnp.zeros_like(cols[0])] * (LANES - len(cols)), axis=-1)
    one = jnp.ones((bs,), F32)
    kcols = [one, one, one, rel, rel, rel] + [jnp.zeros((bs,), F32)] * (LANES - 6)
    kaux = jnp.stack(kcols, axis=-1).astype(BF16)
    return sig, qaux, kaux


def _moba_kernel(sig_ref, q_ref, k_ref, vt_ref, qaux_ref, kaux_ref, o_ref,
                 kmean_ref, kms_ref, sel_ref, acc_ref, m_ref, s_ref, mx_ref):
    hg = pl.program_id(1)
    i = pl.program_id(2)
    bs = MOBA_BLOCK
    nb = k_ref.shape[1] // bs
    hd = MOBA_HEAD_DIM
    va = MOBA_VALUE_ROWS
    n_heads = q_ref.shape[-1] // hd

    @pl.when(i == 0)
    def _():
        for n in range(nb):
            kb = k_ref[0, n * bs:(n + 1) * bs, :].astype(F32)
            kmean_ref[n:n + 1, :] = jnp.sum(kb, axis=0, keepdims=True) * (1.0 / bs)
        for r, piece in enumerate(_split3(kmean_ref[...])):
            kms_ref[r * nb:(r + 1) * nb, :] = piece

    lane = lax.broadcasted_iota(jnp.int32, (1, LANES), 1)
    head_lanes = (lane < hd, lane >= hd)
    pair_lanes = [slice((h // 2) * LANES, (h // 2 + 1) * LANES) for h in range(n_heads)]
    qm = []
    for h in range(n_heads):
        qp = q_ref[0, :, pair_lanes[h]]
        qm.append(jnp.where(head_lanes[h % 2], qp, jnp.zeros_like(qp)))
    qaug = [jnp.concatenate([qm[h], qaux_ref[h]], axis=1) for h in range(n_heads)]
    sig = [sig_ref[hg * n_heads + h] for h in range(n_heads)]
    kaux = kaux_ref[...]

    def select_blocks(h, g):
        nidx = lax.broadcasted_iota(jnp.int32, (nb, bs), 0)
        g = jnp.where(nidx < i, g, NEG_INF)
        sel = jnp.zeros((nb, bs), jnp.bool_)
        for _ in range(MOBA_TOPK):
            best = jnp.max(g, axis=0, keepdims=True)
            first = jnp.min(jnp.where(g == best, nidx, nb), axis=0, keepdims=True)
            pick = nidx == first
            sel = sel | pick
            g = jnp.where(pick, NEG_INF, g)
        sel_ref[h] = jnp.where(sel & (nidx < i), 0.0, NEG_INF)

    def stage_a(blk, slot, with_gate=False):
        start = pl.multiple_of(blk * bs, bs)
        kj = k_ref[0, pl.ds(start, bs), :]
        kaug = [jnp.concatenate([kj[:, pair_lanes[2 * p]], kaux], axis=1) for p in range(n_heads // 2)]
        if with_gate:
            zeros = jnp.zeros((3 * nb, LANES), BF16)
            kaug = [jnp.concatenate([kaug[p], jnp.concatenate([kms_ref[:, pair_lanes[2 * p]], zeros], axis=1)], axis=0)
                    for p in range(n_heads // 2)]
        for h in range(n_heads):
            t = _dot_nt(kaug[h // 2], qaug[h])
            if with_gate:
                select_blocks(h, t[bs:bs + nb] + t[bs + nb:bs + 2 * nb] + t[bs + 2 * nb:])
                t = t[:bs]
            s_ref[slot, h] = t
            mx_ref[slot, h:h + 1, :] = jnp.max(t, axis=0, keepdims=True)

    def values(blk, h):
        start = pl.multiple_of(blk * bs, bs)
        return vt_ref[0, h * va:(h + 1) * va, pl.ds(start, bs)]

    def stage_b_own(slot):
        krow = lax.broadcasted_iota(jnp.int32, (bs, bs), 0)
        qcol = lax.broadcasted_iota(jnp.int32, (bs, bs), 1)
        causal = qcol >= krow
        for h in range(n_heads):
            t = jnp.where(causal, s_ref[slot, h], NEG_INF)
            m = jnp.max(t, axis=0, keepdims=True)
            p = jnp.exp2(t - m)
            m_ref[h:h + 1, :] = m
            acc_ref[h] = _dot(values(i, h), p.astype(BF16))

    def stage_b_past(blk, slot):
        off = ((i - blk) * bs).astype(F32)
        for h in range(n_heads):
            cs = sel_ref[h, pl.ds(blk, 1), :] - sig[h] * off
            m_old = m_ref[h:h + 1, :]
            m_new = jnp.maximum(m_old, mx_ref[slot, h:h + 1, :] + cs)
            alpha = jnp.exp2(m_old - m_new)
            p = jnp.exp2(s_ref[slot, h] + (cs - m_new))
            m_ref[h:h + 1, :] = m_new
            acc_ref[h] = alpha * acc_ref[h] + _dot(values(blk, h), p.astype(BF16))

    stage_a(i, 0, with_gate=True)
    stage_a(jnp.maximum(i - 1, 0), 1)
    stage_b_own(0)

    def step(u, carry):
        blk = i - 1 - 2 * u
        stage_a(blk - 1, 0)
        stage_b_past(blk, 1)
        stage_a(jnp.maximum(blk - 2, 0), 1)
        stage_b_past(blk - 1, 0)
        return carry

    lax.fori_loop(0, i // 2, step, 0)

    @pl.when(i % 2 == 1)
    def _():
        stage_b_past(0, 1)

    for p in range(n_heads // 2):
        ot = jnp.concatenate([acc_ref[2 * p + h, :hd, :] / acc_ref[2 * p + h, hd:hd + 1, :] for h in range(2)], axis=0)
        o_ref[0, :, pair_lanes[2 * p]] = ot.T.astype(BF16)


def _moba(qa, ka, vat, heads_per_step=4):
    B, S, W = qa.shape
    bs = MOBA_BLOCK
    nb = S // bs
    n_heads = W // MOBA_HEAD_DIM
    gw = heads_per_step * MOBA_HEAD_DIM
    sig, qaux, kaux = _alibi_tables(n_heads, bs)
    return pl.pallas_call(
        _moba_kernel,
        grid=(B, W // gw, nb),
        in_specs=[
            pl.BlockSpec(memory_space=pltpu.SMEM),
            pl.BlockSpec((1, bs, gw), lambda b, hg, i: (b, i, hg)),
            pl.BlockSpec((1, S, gw), lambda b, hg, i: (b, 0, hg)),
            pl.BlockSpec((1, heads_per_step * MOBA_VALUE_ROWS, S), lambda b, hg, i: (b, hg, 0)),
            pl.BlockSpec((heads_per_step, bs, LANES), lambda b, hg, i: (hg, 0, 0)),
            pl.BlockSpec((bs, LANES), lambda b, hg, i: (0, 0)),
        ],
        out_specs=pl.BlockSpec((1, bs, gw), lambda b, hg, i: (b, i, hg)),
        out_shape=jax.ShapeDtypeStruct((B, S, W), BF16),
        scratch_shapes=[
            pltpu.VMEM((nb, gw), F32),
            pltpu.VMEM((3 * nb, gw), BF16),
            pltpu.VMEM((heads_per_step, nb, bs), F32),
            pltpu.VMEM((heads_per_step, MOBA_VALUE_ROWS, bs), F32),
            pltpu.VMEM((8, bs), F32),
            pltpu.VMEM((2, heads_per_step, bs, bs), F32),
            pltpu.VMEM((2, 8, bs), F32),
        ],
        compiler_params=pltpu.CompilerParams(
            dimension_semantics=("arbitrary", "arbitrary", "arbitrary"), vmem_limit_bytes=VMEM_LIMIT_BYTES),
        name="moba_attention",
    )(sig, qa, ka, vat, qaux, kaux)


def _gla_kernel(q_ref, k_ref, la_ref, v_ref, r_ref, gn_ref, o_ref, st_ref):
    s_idx = pl.program_id(1)
    C = GLA_CHUNK
    tc = q_ref.shape[1]

    @pl.when(s_idx == 0)
    def _():
        st_ref[...] = jnp.zeros_like(st_ref)

    G = GLA_GROUP
    n_pairs = GLA_HEADS // 2
    lane = lax.broadcasted_iota(jnp.int32, (1, LANES), 1)
    head_lanes = (lane < GLA_KEY_DIM, lane >= GLA_KEY_DIM)
    ti = lax.broadcasted_iota(jnp.int32, (G, G), 0)
    si = lax.broadcasted_iota(jnp.int32, (G, G), 1)
    causal = (ti // C == si // C) & (si <= ti)
    tril = jnp.where(causal, 1.0, 0.0).astype(BF16)
    gn = gn_ref[...]

    cpg = G // C
    n_groups = tc // G
    grp_rows = [slice(g * G, (g + 1) * G) for g in range(n_groups)]
    chunk_rows = [slice(c * C, (c + 1) * C) for c in range(cpg)]
    pair_lanes = [slice(hp * LANES, (hp + 1) * LANES) for hp in range(n_pairs)]
    val_lanes = [slice(hd * GLA_VAL_DIM, (hd + 1) * GLA_VAL_DIM) for hd in range(GLA_HEADS)]

    b = [sum(_dot(tril, piece) for piece in _split3(la_ref[0, rows, :])) for rows in grp_rows]

    q_m, k_inv, k_dec, decay = [], [], [], []
    for g, rows in enumerate(grp_rows):
        b_last = [b[g][c * C + C - 1:c * C + C, :] for c in range(cpg)]
        b_end = jnp.concatenate([jnp.broadcast_to(bl, (C, bl.shape[1])) for bl in b_last], axis=0)
        q_dec = q_ref[0, rows, :] * jnp.exp(b[g])
        ks = k_ref[0, rows, :]
        k_inv.append((ks * jnp.exp(-b[g])).astype(BF16))
        k_dec.append((ks * jnp.exp(b_end - b[g])).astype(BF16))
        decay.append([jnp.exp(bl) for bl in b_last])
        q_m.append([jnp.where(head_lanes[hd % 2], q_dec[:, pair_lanes[hd // 2]], 0.0).astype(BF16)
                    for hd in range(GLA_HEADS)])

    attn = [[_dot_nt(q_m[g][hd], k_inv[g][:, pair_lanes[hd // 2]]) for hd in range(GLA_HEADS)]
            for g in range(n_groups)]

    update = [[[None] * cpg for _ in range(n_pairs)] for _ in range(n_groups)]
    for g, rows in enumerate(grp_rows):
        for hp in range(n_pairs):
            v_pair = v_ref[0, rows, hp * 2 * GLA_VAL_DIM:(hp + 1) * 2 * GLA_VAL_DIM]
            for c, cr in enumerate(chunk_rows):
                u = _dot_tn(v_pair[cr], k_dec[g][cr, pair_lanes[hp]])
                update[g][hp][c] = jnp.where(head_lanes[0], u[:GLA_VAL_DIM], u[GLA_VAL_DIM:])

    o_intra = [[_dot(jnp.where(causal, attn[g][hd], 0.0).astype(BF16), v_ref[0, grp_rows[g], val_lanes[hd]])
                for hd in range(GLA_HEADS)] for g in range(n_groups)]

    states = [st_ref[hp] for hp in range(n_pairs)]
    for g, rows in enumerate(grp_rows):
        o_inter = [[None] * cpg for _ in range(GLA_HEADS)]
        for c, cr in enumerate(chunk_rows):
            for hp in range(n_pairs):
                q_pair = jnp.concatenate([q_m[g][2 * hp][cr], q_m[g][2 * hp + 1][cr]], axis=0)
                oi = _dot_nt(q_pair, states[hp].astype(BF16))
                o_inter[2 * hp][c] = oi[:C]
                o_inter[2 * hp + 1][c] = oi[C:]
                states[hp] = states[hp] * decay[g][c][:, pair_lanes[hp]] + update[g][hp][c]
        for hd in range(GLA_HEADS):
            o = jnp.concatenate(o_inter[hd], axis=0) + o_intra[g][hd]
            ms = jnp.mean(o * o, axis=-1, keepdims=True)
            o_ref[0, rows, val_lanes[hd]] = (o * lax.rsqrt(ms + RMS_EPS) * gn * r_ref[0, rows, val_lanes[hd]]).astype(BF16)

    for hp in range(n_pairs):
        st_ref[hp] = states[hp]


def _gla(qg, kg, la, vg, rg, gnorm, tc=512):
    B, S, kw = qg.shape
    vw = vg.shape[-1]
    row = lambda w: pl.BlockSpec((1, tc, w), lambda b, s: (b, s, 0))
    return pl.pallas_call(
        _gla_kernel,
        grid=(B, S // tc),
        in_specs=[row(kw), row(kw), row(kw), row(vw), row(vw),
                  pl.BlockSpec((1, GLA_VAL_DIM), lambda b, s: (0, 0))],
        out_specs=row(vw),
        out_shape=jax.ShapeDtypeStruct((B, S, vw), BF16),
        scratch_shapes=[pltpu.VMEM((GLA_HEADS // 2, GLA_VAL_DIM, LANES), F32)],
        compiler_params=pltpu.CompilerParams(
            dimension_semantics=("arbitrary", "arbitrary"), vmem_limit_bytes=VMEM_LIMIT_BYTES),
        name="gla_scan",
    )(qg, kg, la, vg, rg, gnorm)


def _memkv_kernel(mem_ref, w_ref, k_ref, v_ref):
    kv = _dot(mem_ref[0].astype(BF16), w_ref[...])
    d = k_ref.shape[-1]
    k_ref[0] = kv[:, :d].astype(BF16)
    v_ref[0] = kv[:, d:].astype(BF16)


def _mem_kv(mem, w_xkv):
    B, M, D = mem.shape
    wb = w_xkv.astype(BF16)
    blk = pl.BlockSpec((1, M, D), lambda b: (b, 0, 0))
    return pl.pallas_call(
        _memkv_kernel,
        grid=(B,),
        in_specs=[blk, pl.BlockSpec(wb.shape, lambda b: (0, 0))],
        out_specs=(blk, blk),
        out_shape=(jax.ShapeDtypeStruct((B, M, D), BF16), jax.ShapeDtypeStruct((B, M, D), BF16)),
        compiler_params=pltpu.CompilerParams(
            dimension_semantics=("arbitrary",), vmem_limit_bytes=VMEM_LIMIT_BYTES),
        name="mem_kv_projection",
    )(mem, wb)


def _post_mixer_kernel(alpha, x_ref, mo_ref, go_ref, wm1_ref, wm2_ref, g1_ref, b1_ref, wxq_ref,
                       km_ref, vm_ref, wxo_ref, g2_ref, b2_ref, o_ref):
    x = x_ref[0]
    mix = _dot(mo_ref[0], wm1_ref[...]) + _dot(go_ref[0], wm2_ref[...])
    x1 = _layer_norm(alpha * x + mix, g1_ref[...], b1_ref[...])
    d = x.shape[-1]
    hd = d // XATTN_HEADS
    q = (_dot(x1.astype(BF16), wxq_ref[...]) * (hd ** -0.5)).astype(BF16)
    xa = jnp.zeros_like(x)
    for h in range(XATTN_HEADS):
        cols = slice(h * hd, (h + 1) * hd)
        s = _dot_nt(q[:, cols], km_ref[0, :, cols])
        m = jnp.max(s, axis=-1, keepdims=True)
        p = jnp.exp(s - m)
        l = jnp.sum(p, axis=-1, keepdims=True)
        oh = _dot(p.astype(BF16), vm_ref[0, :, cols]) / l
        xa = xa + _dot(oh.astype(BF16), wxo_ref[cols, :])
    o_ref[0] = _layer_norm(alpha * x1 + xa, g2_ref[...], b2_ref[...])


def _post_mixer(x, moba_o, gla_o, w_mix_o, ln1_g, ln1_b, w_xq, kmem, vmem, w_xo, ln2_g, ln2_b, alpha, tm=512):
    B, S, D = x.shape
    aw = moba_o.shape[-1]
    M = kmem.shape[1]
    wm1 = w_mix_o[:aw].astype(BF16)
    wm2 = w_mix_o[aw:].astype(BF16)
    wxq = w_xq.astype(BF16)
    wxo = w_xo.astype(BF16)
    vec = lambda a: a.reshape(1, D)
    row = lambda w: pl.BlockSpec((1, tm, w), lambda b, s: (b, s, 0))
    full = lambda a: pl.BlockSpec(a.shape, lambda b, s: (0,) * a.ndim)
    vspec = pl.BlockSpec((1, D), lambda b, s: (0, 0))
    memspec = pl.BlockSpec((1, M, D), lambda b, s: (b, 0, 0))
    return pl.pallas_call(
        functools.partial(_post_mixer_kernel, alpha),
        grid=(B, S // tm),
        in_specs=[row(D), row(aw), row(gla_o.shape[-1]), full(wm1), full(wm2), vspec, vspec, full(wxq),
                  memspec, memspec, full(wxo), vspec, vspec],
        out_specs=row(D),
        out_shape=jax.ShapeDtypeStruct((B, S, D), F32),
        compiler_params=pltpu.CompilerParams(
            dimension_semantics=("arbitrary", "arbitrary"), vmem_limit_bytes=VMEM_LIMIT_BYTES),
        name="post_mixer",
    )(x, moba_o, gla_o, wm1, wm2, vec(ln1_g), vec(ln1_b), wxq, kmem, vmem, wxo, vec(ln2_g), vec(ln2_b))


def _mlp_kernel(alpha, fc, x_ref, w1_ref, w2_ref, g_ref, b_ref, o_ref):
    x = x_ref[0]
    xb = x.astype(BF16)
    acc = jnp.zeros_like(x)
    for c in range(w1_ref.shape[1] // fc):
        cols = slice(c * fc, (c + 1) * fc)
        h = jnp.maximum(_dot(xb, w1_ref[:, cols]), 0.0)
        acc = acc + _dot((h * h).astype(BF16), w2_ref[cols, :])
    o_ref[0] = _layer_norm(alpha * x + acc, g_ref[...], b_ref[...])


def _mlp(x, w_ff1, w_ff2, g, b, alpha, tm=512, fc=512):
    B, S, D = x.shape
    w1 = w_ff1.astype(BF16)
    w2 = w_ff2.astype(BF16)
    row = pl.BlockSpec((1, tm, D), lambda bb, s: (bb, s, 0))
    full = lambda a: pl.BlockSpec(a.shape, lambda bb, s: (0, 0))
    vspec = pl.BlockSpec((1, D), lambda bb, s: (0, 0))
    return pl.pallas_call(
        functools.partial(_mlp_kernel, alpha, fc),
        grid=(B, S // tm),
        in_specs=[row, full(w1), full(w2), vspec, vspec],
        out_specs=row,
        out_shape=jax.ShapeDtypeStruct((B, S, D), F32),
        compiler_params=pltpu.CompilerParams(
            dimension_semantics=("arbitrary", "arbitrary"), vmem_limit_bytes=VMEM_LIMIT_BYTES),
        name="sq_relu_mlp",
    )(x, w1, w2, g.reshape(1, D), b.reshape(1, D))


def kernel(x, mem, w_in, w_gate_up, b_gate, gla_norm_g, w_mix_o, ln1_g, ln1_b, w_xq, w_xkv, w_xo,
           ln2_g, ln2_b, w_ff1, w_ff2, ln3_g, ln3_b):
    depth = w_in.shape[0]
    alpha = (2.0 * depth) ** 0.25
    for l in range(depth):
        qa, ka, vat, qg, kg, vg, rg, la = _in_projection(x, w_in[l], w_gate_up[l], b_gate[l])
        moba_o = _moba(qa, ka, vat)
        gla_o = _gla(qg, kg, la, vg, rg, gla_norm_g[l].reshape(1, GLA_VAL_DIM))
        kmem, vmem = _mem_kv(mem, w_xkv[l])
        x = _post_mixer(x, moba_o, gla_o, w_mix_o[l], ln1_g[l], ln1_b[l], w_xq[l], kmem, vmem, w_xo[l],
                        ln2_g[l], ln2_b[l], alpha)
        x = _mlp(x, w_ff1[l], w_ff2[l], ln3_g[l], ln3_b[l], alpha)
    return x
```

```python
import functools

import jax
import jax.numpy as jnp
from jax import lax
from jax.experimental import pallas as pl
from jax.experimental.pallas import tpu as pltpu

F32 = jnp.float32
BF16 = jnp.bfloat16

MOBA_HEAD_DIM = 64
MOBA_VALUE_ROWS = 80
MOBA_BLOCK = 256
MOBA_TOPK = 3
GLA_HEADS = 4
GLA_KEY_DIM = 64
GLA_VAL_DIM = 128
GLA_GATE_RANK = 16
GLA_GATE_TEMP = 16.0
GLA_CHUNK = 64
GLA_GROUP = 256
XATTN_HEADS = 4
POST_SUB_ROWS = 256
MLP_SUB_ROWS = 512
LN_EPS = 1e-5
RMS_EPS = 1e-6

LANES = 128
VMEM_LIMIT_BYTES = 56 * 1024 * 1024

NEG_INF = float("-inf")
LOG2E = 1.4426950408889634


def _dot(a, b):
    return jnp.dot(a, b, preferred_element_type=F32)


def _dot_nt(a, b):
    return lax.dot_general(a, b, (((1,), (1,)), ((), ())), preferred_element_type=F32)


def _dot_tn(a, b):
    return lax.dot_general(a, b, (((0,), (0,)), ((), ())), preferred_element_type=F32)


def _split3(a):
    hi = a.astype(BF16)
    r1 = a - hi.astype(F32)
    mid = r1.astype(BF16)
    lo = (r1 - mid.astype(F32)).astype(BF16)
    return hi, mid, lo


def _layer_norm(x, g, b):
    mu = jnp.mean(x, axis=-1, keepdims=True)
    xc = x - mu
    var = jnp.mean(xc * xc, axis=-1, keepdims=True)
    return xc * lax.rsqrt(var + LN_EPS) * g + b


def _log_sigmoid(z):
    return jnp.minimum(z, 0.0) - jnp.log(1.0 + jnp.exp(-jnp.abs(z)))


def _sigmoid(z):
    return 1.0 / (1.0 + jnp.exp(-z))


def _inproj_kernel(x_ref, wqk_ref, wvat_ref, wg_ref, wvr_ref, wz_ref, bg_ref,
                   qa_ref, ka_ref, vat_ref, qg_ref, kg_ref, vg_ref, rg_ref, la_ref):
    xb = x_ref[0].astype(BF16)
    aw = qa_ref.shape[-1]
    qk = _dot(xb, wqk_ref[...])
    qa_ref[0] = (qk[:, :aw] * (MOBA_HEAD_DIM ** -0.5 * LOG2E)).astype(BF16)
    ka_ref[0] = qk[:, aw:].astype(BF16)
    vt = _dot_nt(wvat_ref[...], xb).astype(BF16)
    hd, va = MOBA_HEAD_DIM, MOBA_VALUE_ROWS
    for h in range(aw // hd):
        vat_ref[0, h * va:h * va + hd, :] = vt[h * hd:(h + 1) * hd, :]
        vat_ref[0, h * va + hd:(h + 1) * va, :] = jnp.ones((va - hd, vt.shape[1]), BF16)
    kw = qg_ref.shape[-1]
    g = _dot(xb, wg_ref[...])
    qg_ref[0] = g[:, :kw] * (GLA_KEY_DIM ** -0.5)
    kg_ref[0] = g[:, kw:2 * kw]
    vw = vg_ref.shape[-1]
    vr = _dot(xb, wvr_ref[...])
    vg_ref[0] = vr[:, :vw].astype(BF16)
    rg = vr[:, vw:]
    rg_ref[0] = rg * _sigmoid(rg)
    gh, gm, gl = _split3(g[:, 2 * kw:])
    grp = lax.broadcasted_iota(jnp.int32, (1, LANES), 1) // GLA_GATE_RANK
    lhs = jnp.where((grp == 1) | (grp == 4), gm, jnp.where(grp == 2, gl, gh))
    z = _dot(lhs, wz_ref[...]) + bg_ref[...]
    la_ref[0] = _log_sigmoid(z) / GLA_GATE_TEMP


def _in_projection(x, w_in, w_gate_up, b_gate, tm=1024):
    B, S, D = x.shape
    aw = 512
    vrows = aw // MOBA_HEAD_DIM * MOBA_VALUE_ROWS
    kw = GLA_HEADS * GLA_KEY_DIM
    vw = GLA_HEADS * GLA_VAL_DIM
    c = 0
    wq = w_in[:, c:c + aw]; c += aw
    wk = w_in[:, c:c + aw]; c += aw
    wv = w_in[:, c:c + aw]; c += aw
    wqg = w_in[:, c:c + kw]; c += kw
    wkg = w_in[:, c:c + kw]; c += kw
    wvg = w_in[:, c:c + vw]; c += vw
    wrg = w_in[:, c:c + vw]; c += vw
    wglr = w_in[:, c:c + GLA_GATE_RANK]
    wqk = jnp.concatenate([wq, wk], axis=1).astype(BF16)
    wvat = wv.T.astype(BF16)
    n_terms = 6
    pad = LANES - n_terms * GLA_GATE_RANK
    wglr_rep = jnp.pad(jnp.tile(wglr, (1, n_terms)), ((0, 0), (0, pad)))
    wg = jnp.concatenate([wqg, wkg, wglr_rep], axis=1).astype(BF16)
    wvr = jnp.concatenate([wvg, wrg], axis=1).astype(BF16)
    wh, wm, wl = _split3(w_gate_up)
    wz = jnp.pad(jnp.concatenate([wh, wh, wh, wm, wm, wl], axis=0), ((0, pad), (0, 0)))
    bg = b_gate.reshape(1, kw)

    def full(a):
        return pl.BlockSpec(a.shape, lambda b, s: (0,) * a.ndim)

    row = lambda w: pl.BlockSpec((1, tm, w), lambda b, s: (b, s, 0))
    out_shape = (
        jax.ShapeDtypeStruct((B, S, aw), BF16),
        jax.ShapeDtypeStruct((B, S, aw), BF16),
        jax.ShapeDtypeStruct((B, vrows, S), BF16),
        jax.ShapeDtypeStruct((B, S, kw), F32),
        jax.ShapeDtypeStruct((B, S, kw), F32),
        jax.ShapeDtypeStruct((B, S, vw), BF16),
        jax.ShapeDtypeStruct((B, S, vw), F32),
        jax.ShapeDtypeStruct((B, S, kw), F32),
    )
    out_specs = (row(aw), row(aw), pl.BlockSpec((1, vrows, tm), lambda b, s: (b, 0, s)),
                 row(kw), row(kw), row(vw), row(vw), row(kw))
    return pl.pallas_call(
        _inproj_kernel,
        grid=(B, S // tm),
        in_specs=[row(D), full(wqk), full(wvat), full(wg), full(wvr), full(wz), full(bg)],
        out_specs=out_specs,
        out_shape=out_shape,
        compiler_params=pltpu.CompilerParams(
            dimension_semantics=("arbitrary", "arbitrary"), vmem_limit_bytes=VMEM_LIMIT_BYTES),
        name="in_projection",
    )(x, wqk, wvat, wg, wvr, wz, bg)


def _alibi_tables(n_heads, bs):
    slopes = 2.0 ** (-8.0 * jnp.arange(1, n_heads + 1, dtype=F32) / n_heads)
    sig = slopes * LOG2E
    rel = jnp.arange(bs, dtype=F32)
    a = -sig[:, None] * rel[None, :]
    sig_b = jnp.broadcast_to(sig[:, None], a.shape)
    cols = list(_split3(a)) + list(_split3(sig_b))
    qaux = jnp.stack(cols + [jnp.zeros_like(cols[0])] * (LANES - len(cols)), axis=-1)
    one = jnp.ones((bs,), F32)
    kcols = [one, one, one, rel, rel, rel] + [jnp.zeros((bs,), F32)] * (LANES - 6)
    kaux = jnp.stack(kcols, axis=-1).astype(BF16)
    return sig, qaux, kaux


def _moba_kernel(sig_ref, q_ref, k_ref, vt_ref, qaux_ref, kaux_ref, o_ref,
                 kmean_ref, kms_ref, sel_ref, acc_ref, m_ref, s_ref, mx_ref):
    hg = pl.program_id(1)
    i = pl.program_id(2)
    bs = MOBA_BLOCK
    nb = k_ref.shape[1] // bs
    hd = MOBA_HEAD_DIM
    va = MOBA_VALUE_ROWS
    n_heads = q_ref.shape[-1] // hd

    @pl.when(i == 0)
    def _():
        for n in range(nb):
            kb = k_ref[0, n * bs:(n + 1) * bs, :].astype(F32)
            kmean_ref[n:n + 1, :] = jnp.sum(kb, axis=0, keepdims=True) * (1.0 / bs)
        for r, piece in enumerate(_split3(kmean_ref[...])):
            kms_ref[r * nb:(r + 1) * nb, :] = piece

    lane = lax.broadcasted_iota(jnp.int32, (1, LANES), 1)
    head_lanes = (lane < hd, lane >= hd)
    pair_lanes = [slice((h // 2) * LANES, (h // 2 + 1) * LANES) for h in range(n_heads)]
    qm = []
    for h in range(n_heads):
        qp = q_ref[0, :, pair_lanes[h]]
        qm.append(jnp.where(head_lanes[h % 2], qp, jnp.zeros_like(qp)))
    qaug = [jnp.concatenate([qm[h], qaux_ref[h]], axis=1) for h in range(n_heads)]
    sig = [sig_ref[hg * n_heads + h] for h in range(n_heads)]
    kaux = kaux_ref[...]

    def select_blocks(h, g):
        nidx = lax.broadcasted_iota(jnp.int32, (nb, bs), 0)
        g = jnp.where(nidx < i, g, NEG_INF)
        sel = jnp.zeros((nb, bs), jnp.bool_)
        for _ in range(MOBA_TOPK):
            best = jnp.max(g, axis=0, keepdims=True)
            first = jnp.min(jnp.where(g == best, nidx, nb), axis=0, keepdims=True)
            pick = nidx == first
            sel = sel | pick
            g = jnp.where(pick, NEG_INF, g)
        sel_ref[h] = jnp.where(sel & (nidx < i), 0.0, NEG_INF)

    def stage_a(blk, slot, with_gate=False):
        start = pl.multiple_of(blk * bs, bs)
        kj = k_ref[0, pl.ds(start, bs), :]
        kaug = [jnp.concatenate([kj[:, pair_lanes[2 * p]], kaux], axis=1) for p in range(n_heads // 2)]
        if with_gate:
            zeros = jnp.zeros((3 * nb, LANES), BF16)
            kaug = [jnp.concatenate([kaug[p], jnp.concatenate([kms_ref[:, pair_lanes[2 * p]], zeros], axis=1)], axis=0)
                    for p in range(n_heads // 2)]
        for h in range(n_heads):
            t = _dot_nt(kaug[h // 2], qaug[h])
            if with_gate:
                select_blocks(h, t[bs:bs + nb] + t[bs + nb:bs + 2 * nb] + t[bs + 2 * nb:])
                t = t[:bs]
            s_ref[slot, h] = t
            mx_ref[slot, h:h + 1, :] = jnp.max(t, axis=0, keepdims=True)

    def values(blk, h):
        start = pl.multiple_of(blk * bs, bs)
        return vt_ref[0, h * va:(h + 1) * va, pl.ds(start, bs)]

    def stage_b_own(slot):
        krow = lax.broadcasted_iota(jnp.int32, (bs, bs), 0)
        qcol = lax.broadcasted_iota(jnp.int32, (bs, bs), 1)
        causal = qcol >= krow
        for h in range(n_heads):
            t = jnp.where(causal, s_ref[slot, h], NEG_INF)
            m = jnp.max(t, axis=0, keepdims=True)
            p = jnp.exp2(t - m)
            m_ref[h:h + 1, :] = m
            acc_ref[h] = _dot(values(i, h), p.astype(BF16))

    def stage_b_past(blk, slot):
        off = ((i - blk) * bs).astype(F32)
        for h in range(n_heads):
            cs = sel_ref[h, pl.ds(blk, 1), :] - sig[h] * off
            m_old = m_ref[h:h + 1, :]
            m_new = jnp.maximum(m_old, mx_ref[slot, h:h + 1, :] + cs)
            alpha = jnp.exp2(m_old - m_new)
            p = jnp.exp2(s_ref[slot, h] + (cs - m_new))
            m_ref[h:h + 1, :] = m_new
            acc_ref[h] = alpha * acc_ref[h] + _dot(values(blk, h), p.astype(BF16))

    stage_a(i, 0, with_gate=True)
    stage_a(jnp.maximum(i - 1, 0), 1)
    stage_b_own(0)

    def step(u, carry):
        blk = i - 1 - 2 * u
        stage_a(blk - 1, 0)
        stage_b_past(blk, 1)
        stage_a(jnp.maximum(blk - 2, 0), 1)
        stage_b_past(blk - 1, 0)
        return carry

    lax.fori_loop(0, i // 2, step, 0)

    @pl.when(i % 2 == 1)
    def _():
        stage_b_past(0, 1)

    for p in range(n_heads // 2):
        ot = jnp.concatenate([acc_ref[2 * p + h, :hd, :] / acc_ref[2 * p + h, hd:hd + 1, :] for h in range(2)], axis=0)
        o_ref[0, :, pair_lanes[2 * p]] = ot.T.astype(BF16)


def _moba(qa, ka, vat, heads_per_step=8):
    B, S, W = qa.shape
    bs = MOBA_BLOCK
    nb = S // bs
    n_heads = W // MOBA_HEAD_DIM
    gw = heads_per_step * MOBA_HEAD_DIM
    sig, qaux, kaux = _alibi_tables(n_heads, bs)
    return pl.pallas_call(
        _moba_kernel,
        grid=(B, W // gw, nb),
        in_specs=[
            pl.BlockSpec(memory_space=pltpu.SMEM),
            pl.BlockSpec((1, bs, gw), lambda b, hg, i: (b, i, hg)),
            pl.BlockSpec((1, S, gw), lambda b, hg, i: (b, 0, hg)),
            pl.BlockSpec((1, heads_per_step * MOBA_VALUE_ROWS, S), lambda b, hg, i: (b, hg, 0)),
            pl.BlockSpec((heads_per_step, bs, LANES), lambda b, hg, i: (hg, 0, 0)),
            pl.BlockSpec((bs, LANES), lambda b, hg, i: (0, 0)),
        ],
        out_specs=pl.BlockSpec((1, bs, gw), lambda b, hg, i: (b, i, hg)),
        out_shape=jax.ShapeDtypeStruct((B, S, W), BF16),
        scratch_shapes=[
            pltpu.VMEM((nb, gw), F32),
            pltpu.VMEM((3 * nb, gw), BF16),
            pltpu.VMEM((heads_per_step, nb, bs), F32),
            pltpu.VMEM((heads_per_step, MOBA_VALUE_ROWS, bs), F32),
            pltpu.VMEM((8, bs), F32),
            pltpu.VMEM((2, heads_per_step, bs, bs), F32),
            pltpu.VMEM((2, 8, bs), F32),
        ],
        compiler_params=pltpu.CompilerParams(
            dimension_semantics=("arbitrary", "arbitrary", "arbitrary"), vmem_limit_bytes=VMEM_LIMIT_BYTES),
        name="moba_attention",
    )(sig, qa, ka, vat, qaux, kaux)


def _gla_kernel(q_ref, k_ref, la_ref, v_ref, r_ref, gn_ref, o_ref, st_ref):
    s_idx = pl.program_id(1)
    C = GLA_CHUNK
    tc = q_ref.shape[1]

    @pl.when(s_idx == 0)
    def _():
        st_ref[...] = jnp.zeros_like(st_ref)

    G = GLA_GROUP
    n_pairs = GLA_HEADS // 2
    lane = lax.broadcasted_iota(jnp.int32, (1, LANES), 1)
    head_lanes = (lane < GLA_KEY_DIM, lane >= GLA_KEY_DIM)
    ti = lax.broadcasted_iota(jnp.int32, (G, G), 0)
    si = lax.broadcasted_iota(jnp.int32, (G, G), 1)
    causal = (ti // C == si // C) & (si <= ti)
    tril = jnp.where(causal, 1.0, 0.0).astype(BF16)
    gn = gn_ref[...]

    cpg = G // C
    n_groups = tc // G
    grp_rows = [slice(g * G, (g + 1) * G) for g in range(n_groups)]
    chunk_rows = [slice(c * C, (c + 1) * C) for c in range(cpg)]
    pair_lanes = [slice(hp * LANES, (hp + 1) * LANES) for hp in range(n_pairs)]
    val_lanes = [slice(hd * GLA_VAL_DIM, (hd + 1) * GLA_VAL_DIM) for hd in range(GLA_HEADS)]

    b = [sum(_dot(tril, piece) for piece in _split3(la_ref[0, rows, :])) for rows in grp_rows]

    q_m, k_inv, k_dec, decay = [], [], [], []
    for g, rows in enumerate(grp_rows):
        b_last = [b[g][c * C + C - 1:c * C + C, :] for c in range(cpg)]
        b_end = jnp.concatenate([jnp.broadcast_to(bl, (C, bl.shape[1])) for bl in b_last], axis=0)
        q_dec = q_ref[0, rows, :] * jnp.exp(b[g])
        ks = k_ref[0, rows, :]
        k_inv.append((ks * jnp.exp(-b[g])).astype(BF16))
        k_dec.append((ks * jnp.exp(b_end - b[g])).astype(BF16))
        decay.append([jnp.exp(bl) for bl in b_last])
        q_m.append([jnp.where(head_lanes[hd % 2], q_dec[:, pair_lanes[hd // 2]], 0.0).astype(BF16)
                    for hd in range(GLA_HEADS)])

    attn = [[_dot_nt(q_m[g][hd], k_inv[g][:, pair_lanes[hd // 2]]) for hd in range(GLA_HEADS)]
            for g in range(n_groups)]

    update = [[[None] * cpg for _ in range(n_pairs)] for _ in range(n_groups)]
    for g, rows in enumerate(grp_rows):
        for hp in range(n_pairs):
            v_pair = v_ref[0, rows, hp * 2 * GLA_VAL_DIM:(hp + 1) * 2 * GLA_VAL_DIM]
            for c, cr in enumerate(chunk_rows):
                u = _dot_tn(v_pair[cr], k_dec[g][cr, pair_lanes[hp]])
                update[g][hp][c] = jnp.where(head_lanes[0], u[:GLA_VAL_DIM], u[GLA_VAL_DIM:])

    o_intra = [[_dot(jnp.where(causal, attn[g][hd], 0.0).astype(BF16), v_ref[0, grp_rows[g], val_lanes[hd]])
                for hd in range(GLA_HEADS)] for g in range(n_groups)]

    states = [st_ref[hp] for hp in range(n_pairs)]
    for g, rows in enumerate(grp_rows):
        o_inter = [[None] * cpg for _ in range(GLA_HEADS)]
        for c, cr in enumerate(chunk_rows):
            for hp in range(n_pairs):
                q_pair = jnp.concatenate([q_m[g][2 * hp][cr], q_m[g][2 * hp + 1][cr]], axis=0)
                oi = _dot_nt(q_pair, states[hp].astype(BF16))
                o_inter[2 * hp][c] = oi[:C]
                o_inter[2 * hp + 1][c] = oi[C:]
                states[hp] = states[hp] * decay[g][c][:, pair_lanes[hp]] + update[g][hp][c]
        for hd in range(GLA_HEADS):
            o = jnp.concatenate(o_inter[hd], axis=0) + o_intra[g][hd]
            ms = jnp.mean(o * o, axis=-1, keepdims=True)
            o_ref[0, rows, val_lanes[hd]] = (o * lax.rsqrt(ms + RMS_EPS) * gn * r_ref[0, rows, val_lanes[hd]]).astype(BF16)

    for hp in range(n_pairs):
        st_ref[hp] = states[hp]


def _gla(qg, kg, la, vg, rg, gnorm, tc=512):
    B, S, kw = qg.shape
    vw = vg.shape[-1]
    row = lambda w: pl.BlockSpec((1, tc, w), lambda b, s: (b, s, 0))
    return pl.pallas_call(
        _gla_kernel,
        grid=(B, S // tc),
        in_specs=[row(kw), row(kw), row(kw), row(vw), row(vw),
                  pl.BlockSpec((1, GLA_VAL_DIM), lambda b, s: (0, 0))],
        out_specs=row(vw),
        out_shape=jax.ShapeDtypeStruct((B, S, vw), BF16),
        scratch_shapes=[pltpu.VMEM((GLA_HEADS // 2, GLA_VAL_DIM, LANES), F32)],
        compiler_params=pltpu.CompilerParams(
            dimension_semantics=("arbitrary", "arbitrary"), vmem_limit_bytes=VMEM_LIMIT_BYTES),
        name="gla_scan",
    )(qg, kg, la, vg, rg, gnorm)


def _memkv_kernel(mem_ref, w_ref, k_ref, v_ref):
    kv = _dot(mem_ref[0].astype(BF16), w_ref[...])
    d = k_ref.shape[-1]
    k_ref[0] = kv[:, :d].astype(BF16)
    v_ref[0] = kv[:, d:].astype(BF16)


def _mem_kv(mem, w_xkv):
    B, M, D = mem.shape
    wb = w_xkv.astype(BF16)
    blk = pl.BlockSpec((1, M, D), lambda b: (b, 0, 0))
    return pl.pallas_call(
        _memkv_kernel,
        grid=(B,),
        in_specs=[blk, pl.BlockSpec(wb.shape, lambda b: (0, 0))],
        out_specs=(blk, blk),
        out_shape=(jax.ShapeDtypeStruct((B, M, D), BF16), jax.ShapeDtypeStruct((B, M, D), BF16)),
        compiler_params=pltpu.CompilerParams(
            dimension_semantics=("arbitrary",), vmem_limit_bytes=VMEM_LIMIT_BYTES),
        name="mem_kv_projection",
    )(mem, wb)


def _post_mixer_kernel(alpha, x_ref, mo_ref, go_ref, wm1_ref, wm2_ref, g1_ref, b1_ref, wxq_ref,
                       km_ref, vm_ref, wxo_ref, g2_ref, b2_ref, o_ref):
    tm, d = x_ref.shape[1], x_ref.shape[2]
    hd = d // XATTN_HEADS
    subs = [slice(r * POST_SUB_ROWS, (r + 1) * POST_SUB_ROWS) for r in range(tm // POST_SUB_ROWS)]
    heads = [slice(h * hd, (h + 1) * hd) for h in range(XATTN_HEADS)]
    mix = [_dot(mo_ref[0, r, :], wm1_ref[...]) + _dot(go_ref[0, r, :], wm2_ref[...]) for r in subs]
    x1 = [_layer_norm(alpha * x_ref[0, r, :] + mix[i], g1_ref[...], b1_ref[...]) for i, r in enumerate(subs)]
    q = [(_dot(x1[i].astype(BF16), wxq_ref[...]) * (hd ** -0.5)).astype(BF16) for i in range(len(subs))]
    s = [[_dot_nt(q[i][:, c], km_ref[0, :, c]) for c in heads] for i in range(len(subs))]
    oh = []
    for i in range(len(subs)):
        row = []
        for h, c in enumerate(heads):
            m = jnp.max(s[i][h], axis=-1, keepdims=True)
            p = jnp.exp(s[i][h] - m)
            l = jnp.sum(p, axis=-1, keepdims=True)
            row.append((_dot(p.astype(BF16), vm_ref[0, :, c]) / l).astype(BF16))
        oh.append(row)
    for i, r in enumerate(subs):
        xa = sum(_dot(oh[i][h], wxo_ref[c, :]) for h, c in enumerate(heads))
        o_ref[0, r, :] = _layer_norm(alpha * x1[i] + xa, g2_ref[...], b2_ref[...])


def _post_mixer(x, moba_o, gla_o, w_mix_o, ln1_g, ln1_b, w_xq, kmem, vmem, w_xo, ln2_g, ln2_b, alpha, tm=1024):
    B, S, D = x.shape
    aw = moba_o.shape[-1]
    M = kmem.shape[1]
    wm1 = w_mix_o[:aw].astype(BF16)
    wm2 = w_mix_o[aw:].astype(BF16)
    wxq = w_xq.astype(BF16)
    wxo = w_xo.astype(BF16)
    vec = lambda a: a.reshape(1, D)
    row = lambda w: pl.BlockSpec((1, tm, w), lambda b, s: (b, s, 0))
    full = lambda a: pl.BlockSpec(a.shape, lambda b, s: (0,) * a.ndim)
    vspec = pl.BlockSpec((1, D), lambda b, s: (0, 0))
    memspec = pl.BlockSpec((1, M, D), lambda b, s: (b, 0, 0))
    return pl.pallas_call(
        functools.partial(_post_mixer_kernel, alpha),
        grid=(B, S // tm),
        in_specs=[row(D), row(aw), row(gla_o.shape[-1]), full(wm1), full(wm2), vspec, vspec, full(wxq),
                  memspec, memspec, full(wxo), vspec, vspec],
        out_specs=row(D),
        out_shape=jax.ShapeDtypeStruct((B, S, D), F32),
        compiler_params=pltpu.CompilerParams(
            dimension_semantics=("arbitrary", "arbitrary"), vmem_limit_bytes=VMEM_LIMIT_BYTES),
        name="post_mixer",
    )(x, moba_o, gla_o, wm1, wm2, vec(ln1_g), vec(ln1_b), wxq, kmem, vmem, wxo, vec(ln2_g), vec(ln2_b))


def _mlp_kernel(alpha, fc, x_ref, w1_ref, w2_ref, g_ref, b_ref, o_ref):
    for r in range(x_ref.shape[1] // MLP_SUB_ROWS):
        rows = slice(r * MLP_SUB_ROWS, (r + 1) * MLP_SUB_ROWS)
        x = x_ref[0, rows, :]
        xb = x.astype(BF16)
        acc = jnp.zeros_like(x)
        for c in range(w1_ref.shape[1] // fc):
            cols = slice(c * fc, (c + 1) * fc)
            h = jnp.maximum(_dot(xb, w1_ref[:, cols]), 0.0)
            acc = acc + _dot((h * h).astype(BF16), w2_ref[cols, :])
        o_ref[0, rows, :] = _layer_norm(alpha * x + acc, g_ref[...], b_ref[...])


def _mlp(x, w_ff1, w_ff2, g, b, alpha, tm=1024, fc=512):
    B, S, D = x.shape
    w1 = w_ff1.astype(BF16)
    w2 = w_ff2.astype(BF16)
    row = pl.BlockSpec((1, tm, D), lambda bb, s: (bb, s, 0))
    full = lambda a: pl.BlockSpec(a.shape, lambda bb, s: (0, 0), pipeline_mode=pl.Buffered(1))
    vspec = pl.BlockSpec((1, D), lambda bb, s: (0, 0))
    return pl.pallas_call(
        functools.partial(_mlp_kernel, alpha, fc),
        grid=(B, S // tm),
        in_specs=[row, full(w1), full(w2), vspec, vspec],
        out_specs=row,
        out_shape=jax.ShapeDtypeStruct((B, S, D), F32),
        compiler_params=pltpu.CompilerParams(
            dimension_semantics=("arbitrary", "arbitrary"), vmem_limit_bytes=VMEM_LIMIT_BYTES),
        name="sq_relu_mlp",
    )(x, w1, w2, g.reshape(1, D), b.reshape(1, D))


def kernel(x, mem, w_in, w_gate_up, b_gate, gla_norm_g, w_mix_o, ln1_g, ln1_b, w_xq, w_xkv, w_xo,
           ln2_g, ln2_b, w_ff1, w_ff2, ln3_g, ln3_b):
    depth = w_in.shape[0]
    alpha = (2.0 * depth) ** 0.25
    for l in range(depth):
        qa, ka, vat, qg, kg, vg, rg, la = _in_projection(x, w_in[l], w_gate_up[l], b_gate[l])
        moba_o = _moba(qa, ka, vat)
        gla_o = _gla(qg, kg, la, vg, rg, gla_norm_g[l].reshape(1, GLA_VAL_DIM))
        kmem, vmem = _mem_kv(mem, w_xkv[l])
        x = _post_mixer(x, moba_o, gla_o, w_mix_o[l], ln1_g[l], ln1_b[l], w_xq[l], kmem, vmem, w_xo[l],
                        ln2_g[l], ln2_b[l], alpha)
        x = _mlp(x, w_ff1[l], w_ff2[l], ln3_g[l], ln3_b[l], alpha)
    return x
```

```python
import functools

import jax
import jax.numpy as jnp
from jax import lax
from jax.experimental import pallas as pl
from jax.experimental.pallas import tpu as pltpu

F32 = jnp.float32
BF16 = jnp.bfloat16

MOBA_HEAD_DIM = 64
MOBA_VALUE_ROWS = 80
MOBA_BLOCK = 256
MOBA_TOPK = 3
GLA_HEADS = 4
GLA_KEY_DIM = 64
GLA_VAL_DIM = 128
GLA_GATE_RANK = 16
GLA_GATE_TEMP = 16.0
GLA_CHUNK = 64
GLA_GROUP = 256
XATTN_HEADS = 4
POST_SUB_ROWS = 256
MLP_SUB_ROWS = 512
LN_EPS = 1e-5
RMS_EPS = 1e-6

LANES = 128
VMEM_LIMIT_BYTES = 56 * 1024 * 1024

NEG_INF = float("-inf")
LOG2E = 1.4426950408889634


def _dot(a, b):
    return jnp.dot(a, b, preferred_element_type=F32)


def _dot_nt(a, b):
    return lax.dot_general(a, b, (((1,), (1,)), ((), ())), preferred_element_type=F32)


def _dot_tn(a, b):
    return lax.dot_general(a, b, (((0,), (0,)), ((), ())), preferred_element_type=F32)


def _split3(a):
    hi = a.astype(BF16)
    r1 = a - hi.astype(F32)
    mid = r1.astype(BF16)
    lo = (r1 - mid.astype(F32)).astype(BF16)
    return hi, mid, lo


def _layer_norm(x, g, b):
    mu = jnp.mean(x, axis=-1, keepdims=True)
    xc = x - mu
    var = jnp.mean(xc * xc, axis=-1, keepdims=True)
    return xc * lax.rsqrt(var + LN_EPS) * g + b


def _log_sigmoid(z):
    return jnp.minimum(z, 0.0) - jnp.log(1.0 + jnp.exp(-jnp.abs(z)))


def _sigmoid(z):
    return 1.0 / (1.0 + jnp.exp(-z))


def _inproj_kernel(x_ref, wqk_ref, wvat_ref, wg_ref, wvr_ref, wz_ref, bg_ref,
                   qa_ref, ka_ref, vat_ref, qg_ref, kg_ref, vg_ref, rg_ref, la_ref):
    xb = x_ref[0].astype(BF16)
    aw = qa_ref.shape[-1]
    qk = _dot(xb, wqk_ref[...])
    qa_ref[0] = (qk[:, :aw] * (MOBA_HEAD_DIM ** -0.5 * LOG2E)).astype(BF16)
    ka_ref[0] = qk[:, aw:].astype(BF16)
    vt = _dot_nt(wvat_ref[...], xb).astype(BF16)
    hd, va = MOBA_HEAD_DIM, MOBA_VALUE_ROWS
    for h in range(aw // hd):
        vat_ref[0, h * va:h * va + hd, :] = vt[h * hd:(h + 1) * hd, :]
        vat_ref[0, h * va + hd:(h + 1) * va, :] = jnp.ones((va - hd, vt.shape[1]), BF16)
    kw = qg_ref.shape[-1]
    g = _dot(xb, wg_ref[...])
    qg_ref[0] = g[:, :kw] * (GLA_KEY_DIM ** -0.5)
    kg_ref[0] = g[:, kw:2 * kw]
    vw = vg_ref.shape[-1]
    vr = _dot(xb, wvr_ref[...])
    vg_ref[0] = vr[:, :vw].astype(BF16)
    rg = vr[:, vw:]
    rg_ref[0] = rg * _sigmoid(rg)
    gh, gm, gl = _split3(g[:, 2 * kw:])
    grp = lax.broadcasted_iota(jnp.int32, (1, LANES), 1) // GLA_GATE_RANK
    lhs = jnp.where((grp == 1) | (grp == 4), gm, jnp.where(grp == 2, gl, gh))
    z = _dot(lhs, wz_ref[...]) + bg_ref[...]
    la_ref[0] = _log_sigmoid(z) / GLA_GATE_TEMP


def _in_projection(x, w_in, w_gate_up, b_gate, tm=1024):
    B, S, D = x.shape
    aw = 512
    vrows = aw // MOBA_HEAD_DIM * MOBA_VALUE_ROWS
    kw = GLA_HEADS * GLA_KEY_DIM
    vw = GLA_HEADS * GLA_VAL_DIM
    c = 0
    wq = w_in[:, c:c + aw]; c += aw
    wk = w_in[:, c:c + aw]; c += aw
    wv = w_in[:, c:c + aw]; c += aw
    wqg = w_in[:, c:c + kw]; c += kw
    wkg = w_in[:, c:c + kw]; c += kw
    wvg = w_in[:, c:c + vw]; c += vw
    wrg = w_in[:, c:c + vw]; c += vw
    wglr = w_in[:, c:c + GLA_GATE_RANK]
    wqk = jnp.concatenate([wq, wk], axis=1).astype(BF16)
    wvat = wv.T.astype(BF16)
    n_terms = 6
    pad = LANES - n_terms * GLA_GATE_RANK
    wglr_rep = jnp.pad(jnp.tile(wglr, (1, n_terms)), ((0, 0), (0, pad)))
    wg = jnp.concatenate([wqg, wkg, wglr_rep], axis=1).astype(BF16)
    wvr = jnp.concatenate([wvg, wrg], axis=1).astype(BF16)
    wh, wm, wl = _split3(w_gate_up)
    wz = jnp.pad(jnp.concatenate([wh, wh, wh, wm, wm, wl], axis=0), ((0, pad), (0, 0)))
    bg = b_gate.reshape(1, kw)

    def full(a):
        return pl.BlockSpec(a.shape, lambda b, s: (0,) * a.ndim)

    row = lambda w: pl.BlockSpec((1, tm, w), lambda b, s: (b, s, 0))
    out_shape = (
        jax.ShapeDtypeStruct((B, S, aw), BF16),
        jax.ShapeDtypeStruct((B, S, aw), BF16),
        jax.ShapeDtypeStruct((B, vrows, S), BF16),
        jax.ShapeDtypeStruct((B, S, kw), F32),
        jax.ShapeDtypeStruct((B, S, kw), F32),
        jax.ShapeDtypeStruct((B, S, vw), BF16),
        jax.ShapeDtypeStruct((B, S, vw), F32),
        jax.ShapeDtypeStruct((B, S, kw), F32),
    )
    out_specs = (row(aw), row(aw), pl.BlockSpec((1, vrows, tm), lambda b, s: (b, 0, s)),
                 row(kw), row(kw), row(vw), row(vw), row(kw))
    return pl.pallas_call(
        _inproj_kernel,
        grid=(B, S // tm),
        in_specs=[row(D), full(wqk), full(wvat), full(wg), full(wvr), full(wz), full(bg)],
        out_specs=out_specs,
        out_shape=out_shape,
        compiler_params=pltpu.CompilerParams(
            dimension_semantics=("arbitrary", "arbitrary"), vmem_limit_bytes=VMEM_LIMIT_BYTES),
        name="in_projection",
    )(x, wqk, wvat, wg, wvr, wz, bg)


def _alibi_tables(n_heads, bs):
    slopes = 2.0 ** (-8.0 * jnp.arange(1, n_heads + 1, dtype=F32) / n_heads)
    sig = slopes * LOG2E
    rel = jnp.arange(bs, dtype=F32)
    a = -sig[:, None] * rel[None, :]
    sig_b = jnp.broadcast_to(sig[:, None], a.shape)
    cols = jnp.stack(list(_split3(a)) + list(_split3(sig_b)), axis=-1)
    qaux = jnp.concatenate([cols[0::2], cols[1::2]], axis=-1)
    qaux = jnp.pad(qaux, ((0, 0), (0, 0), (0, LANES - qaux.shape[-1])))
    one = jnp.ones((bs,), F32)
    kcols = jnp.stack([one, one, one, rel, rel, rel], axis=-1)
    zeros = jnp.zeros_like(kcols)
    kaux = jnp.stack([jnp.concatenate([kcols, zeros], axis=-1), jnp.concatenate([zeros, kcols], axis=-1)])
    kaux = jnp.pad(kaux, ((0, 0), (0, 0), (0, LANES - kaux.shape[-1]))).astype(BF16)
    return sig, qaux, kaux


def _moba_kernel(sig_ref, q_ref, k_ref, vt_ref, qaux_ref, kaux_ref, o_ref,
                 kmean_ref, kaug_ref, kms_ref, sel_ref, acc_ref, m_ref, s_ref, mx_ref):
    hg = pl.program_id(1)
    i = pl.program_id(2)
    bs = MOBA_BLOCK
    nb = k_ref.shape[1] // bs
    hd = MOBA_HEAD_DIM
    va = MOBA_VALUE_ROWS
    n_heads = q_ref.shape[-1] // hd
    n_pairs = n_heads // 2
    pair_lanes = [slice(p * LANES, (p + 1) * LANES) for p in range(n_pairs)]
    gate_rows = 3 * nb

    @pl.when(i == 0)
    def _():
        lane = lax.broadcasted_iota(jnp.int32, (1, LANES), 1)
        first = lane < hd
        for n in range(nb):
            kb = k_ref[0, n * bs:(n + 1) * bs, :]
            kmean_ref[n:n + 1, :] = jnp.sum(kb.astype(F32), axis=0, keepdims=True) * (1.0 / bs)
            for p in range(n_pairs):
                kp = kb[:, pair_lanes[p]]
                zero = jnp.zeros_like(kp)
                for h, kh in enumerate((jnp.where(first, kp, zero), jnp.where(first, zero, kp))):
                    rows = slice((2 * n + h) * bs, (2 * n + h + 1) * bs)
                    kaug_ref[p, rows, :] = jnp.concatenate([kh, kaux_ref[h]], axis=1)
        pieces = _split3(kmean_ref[...])
        for p in range(n_pairs):
            for h in range(2):
                for r, piece in enumerate(pieces):
                    pp = piece[:, pair_lanes[p]]
                    zero = jnp.zeros_like(pp)
                    ph = jnp.where(first, pp, zero) if h == 0 else jnp.where(first, zero, pp)
                    rows = slice(h * gate_rows + r * nb, h * gate_rows + (r + 1) * nb)
                    kms_ref[p, rows, :] = jnp.concatenate([ph, zero], axis=1)

    qaug = [jnp.concatenate([q_ref[0, :, pair_lanes[p]], qaux_ref[p]], axis=1) for p in range(n_pairs)]
    sig = [sig_ref[hg * n_heads + h] for h in range(n_heads)]

    def select_blocks(h, g):
        nidx = lax.broadcasted_iota(jnp.int32, (nb, bs), 0)
        g = jnp.where(nidx < i, g, NEG_INF)
        sel = jnp.zeros((nb, bs), jnp.bool_)
        for _ in range(MOBA_TOPK):
            best = jnp.max(g, axis=0, keepdims=True)
            first = jnp.min(jnp.where(g == best, nidx, nb), axis=0, keepdims=True)
            pick = nidx == first
            sel = sel | pick
            g = jnp.where(pick, NEG_INF, g)
        sel_ref[h] = jnp.where(sel & (nidx < i), 0.0, NEG_INF)

    def stage_a(blk, slot, p, with_gate=False):
        start = pl.multiple_of(blk * (2 * bs), 2 * bs)
        lhs = kaug_ref[p, pl.ds(start, 2 * bs), :]
        if with_gate:
            lhs = jnp.concatenate([lhs, kms_ref[p]], axis=0)
        t = _dot_nt(lhs, qaug[p])
        for h in range(2):
            if with_gate:
                g0 = 2 * bs + h * gate_rows
                select_blocks(2 * p + h, t[g0:g0 + nb] + t[g0 + nb:g0 + 2 * nb] + t[g0 + 2 * nb:g0 + 3 * nb])
            th = t[h * bs:(h + 1) * bs]
            s_ref[slot, 2 * p + h] = th
            mx_ref[slot, 2 * p + h:2 * p + h + 1, :] = jnp.max(th, axis=0, keepdims=True)

    def values(blk, h):
        start = pl.multiple_of(blk * bs, bs)
        return vt_ref[0, h * va:(h + 1) * va, pl.ds(start, bs)]

    def stage_b_own(slot, p):
        krow = lax.broadcasted_iota(jnp.int32, (bs, bs), 0)
        qcol = lax.broadcasted_iota(jnp.int32, (bs, bs), 1)
        causal = qcol >= krow
        for h in (2 * p, 2 * p + 1):
            t = jnp.where(causal, s_ref[slot, h], NEG_INF)
            m = jnp.max(t, axis=0, keepdims=True)
            pr = jnp.exp2(t - m)
            m_ref[h:h + 1, :] = m
            acc_ref[h] = _dot(values(i, h), pr.astype(BF16))

    def stage_b_past(blk, slot, p):
        off = ((i - blk) * bs).astype(F32)
        for h in (2 * p, 2 * p + 1):
            cs = sel_ref[h, pl.ds(blk, 1), :] - sig[h] * off
            m_old = m_ref[h:h + 1, :]
            m_new = jnp.maximum(m_old, mx_ref[slot, h:h + 1, :] + cs)
            alpha = jnp.exp2(m_old - m_new)
            pr = jnp.exp2(s_ref[slot, h] + (cs - m_new))
            m_ref[h:h + 1, :] = m_new
            acc_ref[h] = alpha * acc_ref[h] + _dot(values(blk, h), pr.astype(BF16))

    for p in range(n_pairs):
        stage_a(i, 0, p, with_gate=True)
    for p in range(n_pairs):
        stage_a(jnp.maximum(i - 1, 0), 1, p)
        stage_b_own(0, p)

    def step(u, carry):
        blk = i - 1 - 2 * u
        for p in range(n_pairs):
            stage_a(blk - 1, 0, p)
            stage_b_past(blk, 1, p)
        for p in range(n_pairs):
            stage_a(jnp.maximum(blk - 2, 0), 1, p)
            stage_b_past(blk - 1, 0, p)
        return carry

    lax.fori_loop(0, i // 2, step, 0)

    @pl.when(i % 2 == 1)
    def _():
        for p in range(n_pairs):
            stage_b_past(0, 1, p)

    for p in range(n_pairs):
        ot = jnp.concatenate([acc_ref[2 * p + h, :hd, :] / acc_ref[2 * p + h, hd:hd + 1, :] for h in range(2)], axis=0)
        o_ref[0, :, pair_lanes[p]] = ot.T.astype(BF16)


def _moba(qa, ka, vat, heads_per_step=8):
    B, S, W = qa.shape
    bs = MOBA_BLOCK
    nb = S // bs
    n_heads = W // MOBA_HEAD_DIM
    gw = heads_per_step * MOBA_HEAD_DIM
    sig, qaux, kaux = _alibi_tables(n_heads, bs)
    return pl.pallas_call(
        _moba_kernel,
        grid=(B, W // gw, nb),
        in_specs=[
            pl.BlockSpec(memory_space=pltpu.SMEM),
            pl.BlockSpec((1, bs, gw), lambda b, hg, i: (b, i, hg)),
            pl.BlockSpec((1, S, gw), lambda b, hg, i: (b, 0, hg)),
            pl.BlockSpec((1, heads_per_step * MOBA_VALUE_ROWS, S), lambda b, hg, i: (b, hg, 0)),
            pl.BlockSpec((heads_per_step // 2, bs, LANES), lambda b, hg, i: (hg, 0, 0)),
            pl.BlockSpec((2, bs, LANES), lambda b, hg, i: (0, 0, 0)),
        ],
        out_specs=pl.BlockSpec((1, bs, gw), lambda b, hg, i: (b, i, hg)),
        out_shape=jax.ShapeDtypeStruct((B, S, W), BF16),
        scratch_shapes=[
            pltpu.VMEM((nb, gw), F32),
            pltpu.VMEM((heads_per_step // 2, 2 * S, 2 * LANES), BF16),
            pltpu.VMEM((heads_per_step // 2, 6 * nb, 2 * LANES), BF16),
            pltpu.VMEM((heads_per_step, nb, bs), F32),
            pltpu.VMEM((heads_per_step, MOBA_VALUE_ROWS, bs), F32),
            pltpu.VMEM((8, bs), F32),
            pltpu.VMEM((2, heads_per_step, bs, bs), F32),
            pltpu.VMEM((2, 8, bs), F32),
        ],
        compiler_params=pltpu.CompilerParams(
            dimension_semantics=("arbitrary", "arbitrary", "arbitrary"), vmem_limit_bytes=VMEM_LIMIT_BYTES),
        name="moba_attention",
    )(sig, qa, ka, vat, qaux, kaux)


def _gla_kernel(q_ref, k_ref, la_ref, v_ref, r_ref, gn_ref, o_ref, st_ref):
    s_idx = pl.program_id(1)
    C = GLA_CHUNK
    tc = q_ref.shape[1]

    @pl.when(s_idx == 0)
    def _():
        st_ref[...] = jnp.zeros_like(st_ref)

    G = GLA_GROUP
    n_pairs = GLA_HEADS // 2
    lane = lax.broadcasted_iota(jnp.int32, (1, LANES), 1)
    head_lanes = (lane < GLA_KEY_DIM, lane >= GLA_KEY_DIM)
    ti = lax.broadcasted_iota(jnp.int32, (G, G), 0)
    si = lax.broadcasted_iota(jnp.int32, (G, G), 1)
    causal = (ti // C == si // C) & (si <= ti)
    tril = jnp.where(causal, 1.0, 0.0).astype(BF16)
    gn = gn_ref[...]

    cpg = G // C
    n_groups = tc // G
    grp_rows = [slice(g * G, (g + 1) * G) for g in range(n_groups)]
    chunk_rows = [slice(c * C, (c + 1) * C) for c in range(cpg)]
    pair_lanes = [slice(hp * LANES, (hp + 1) * LANES) for hp in range(n_pairs)]
    val_lanes = [slice(hd * GLA_VAL_DIM, (hd + 1) * GLA_VAL_DIM) for hd in range(GLA_HEADS)]

    b = [sum(_dot(tril, piece) for piece in _split3(la_ref[0, rows, :])) for rows in grp_rows]

    q_m, k_inv, k_dec, decay = [], [], [], []
    for g, rows in enumerate(grp_rows):
        b_last = [b[g][c * C + C - 1:c * C + C, :] for c in range(cpg)]
        b_end = jnp.concatenate([jnp.broadcast_to(bl, (C, bl.shape[1])) for bl in b_last], axis=0)
        q_dec = q_ref[0, rows, :] * jnp.exp(b[g])
        ks = k_ref[0, rows, :]
        k_inv.append((ks * jnp.exp(-b[g])).astype(BF16))
        k_dec.append((ks * jnp.exp(b_end - b[g])).astype(BF16))
        decay.append([jnp.exp(bl) for bl in b_last])
        q_m.append([jnp.where(head_lanes[hd % 2], q_dec[:, pair_lanes[hd // 2]], 0.0).astype(BF16)
                    for hd in range(GLA_HEADS)])

    attn = [[_dot_nt(q_m[g][hd], k_inv[g][:, pair_lanes[hd // 2]]) for hd in range(GLA_HEADS)]
            for g in range(n_groups)]

    update = [[[None] * cpg for _ in range(n_pairs)] for _ in range(n_groups)]
    for g, rows in enumerate(grp_rows):
        for hp in range(n_pairs):
            v_pair = v_ref[0, rows, hp * 2 * GLA_VAL_DIM:(hp + 1) * 2 * GLA_VAL_DIM]
            for c, cr in enumerate(chunk_rows):
                u = _dot_tn(v_pair[cr], k_dec[g][cr, pair_lanes[hp]])
                update[g][hp][c] = jnp.where(head_lanes[0], u[:GLA_VAL_DIM], u[GLA_VAL_DIM:])

    o_intra = [[_dot(jnp.where(causal, attn[g][hd], 0.0).astype(BF16), v_ref[0, grp_rows[g], val_lanes[hd]])
                for hd in range(GLA_HEADS)] for g in range(n_groups)]

    states = [st_ref[hp] for hp in range(n_pairs)]
    for g, rows in enumerate(grp_rows):
        o_inter = [[None] * cpg for _ in range(GLA_HEADS)]
        for c, cr in enumerate(chunk_rows):
            for hp in range(n_pairs):
                q_pair = jnp.concatenate([q_m[g][2 * hp][cr], q_m[g][2 * hp + 1][cr]], axis=0)
                oi = _dot_nt(q_pair, states[hp].astype(BF16))
                o_inter[2 * hp][c] = oi[:C]
                o_inter[2 * hp + 1][c] = oi[C:]
                states[hp] = states[hp] * decay[g][c][:, pair_lanes[hp]] + update[g][hp][c]
        for hd in range(GLA_HEADS):
            o = jnp.concatenate(o_inter[hd], axis=0) + o_intra[g][hd]
            ms = jnp.mean(o * o, axis=-1, keepdims=True)
            o_ref[0, rows, val_lanes[hd]] = (o * lax.rsqrt(ms + RMS_EPS) * gn * r_ref[0, rows, val_lanes[hd]]).astype(BF16)

    for hp in range(n_pairs):
        st_ref[hp] = states[hp]


def _gla(qg, kg, la, vg, rg, gnorm, tc=512):
    B, S, kw = qg.shape
    vw = vg.shape[-1]
    row = lambda w: pl.BlockSpec((1, tc, w), lambda b, s: (b, s, 0))
    return pl.pallas_call(
        _gla_kernel,
        grid=(B, S // tc),
        in_specs=[row(kw), row(kw), row(kw), row(vw), row(vw),
                  pl.BlockSpec((1, GLA_VAL_DIM), lambda b, s: (0, 0))],
        out_specs=row(vw),
        out_shape=jax.ShapeDtypeStruct((B, S, vw), BF16),
        scratch_shapes=[pltpu.VMEM((GLA_HEADS // 2, GLA_VAL_DIM, LANES), F32)],
        compiler_params=pltpu.CompilerParams(
            dimension_semantics=("arbitrary", "arbitrary"), vmem_limit_bytes=VMEM_LIMIT_BYTES),
        name="gla_scan",
    )(qg, kg, la, vg, rg, gnorm)


def _memkv_kernel(mem_ref, w_ref, k_ref, v_ref):
    kv = _dot(mem_ref[0].astype(BF16), w_ref[...])
    d = k_ref.shape[-1]
    k_ref[0] = kv[:, :d].astype(BF16)
    v_ref[0] = kv[:, d:].astype(BF16)


def _mem_kv(mem, w_xkv):
    B, M, D = mem.shape
    wb = w_xkv.astype(BF16)
    blk = pl.BlockSpec((1, M, D), lambda b: (b, 0, 0))
    return pl.pallas_call(
        _memkv_kernel,
        grid=(B,),
        in_specs=[blk, pl.BlockSpec(wb.shape, lambda b: (0, 0))],
        out_specs=(blk, blk),
        out_shape=(jax.ShapeDtypeStruct((B, M, D), BF16), jax.ShapeDtypeStruct((B, M, D), BF16)),
        compiler_params=pltpu.CompilerParams(
            dimension_semantics=("arbitrary",), vmem_limit_bytes=VMEM_LIMIT_BYTES),
        name="mem_kv_projection",
    )(mem, wb)


def _post_mixer_kernel(alpha, x_ref, mo_ref, go_ref, wm1_ref, wm2_ref, g1_ref, b1_ref, wxq_ref,
                       km_ref, vm_ref, wxo_ref, g2_ref, b2_ref, o_ref):
    tm, d = x_ref.shape[1], x_ref.shape[2]
    hd = d // XATTN_HEADS
    subs = [slice(r * POST_SUB_ROWS, (r + 1) * POST_SUB_ROWS) for r in range(tm // POST_SUB_ROWS)]
    heads = [slice(h * hd, (h + 1) * hd) for h in range(XATTN_HEADS)]
    mix = [_dot(mo_ref[0, r, :], wm1_ref[...]) + _dot(go_ref[0, r, :], wm2_ref[...]) for r in subs]
    x1 = [_layer_norm(alpha * x_ref[0, r, :] + mix[i], g1_ref[...], b1_ref[...]) for i, r in enumerate(subs)]
    q = [(_dot(x1[i].astype(BF16), wxq_ref[...]) * (hd ** -0.5)).astype(BF16) for i in range(len(subs))]
    s = [[_dot_nt(q[i][:, c], km_ref[0, :, c]) for c in heads] for i in range(len(subs))]
    oh = []
    for i in range(len(subs)):
        row = []
        for h, c in enumerate(heads):
            m = jnp.max(s[i][h], axis=-1, keepdims=True)
            p = jnp.exp(s[i][h] - m)
            l = jnp.sum(p, axis=-1, keepdims=True)
            row.append((_dot(p.astype(BF16), vm_ref[0, :, c]) / l).astype(BF16))
        oh.append(row)
    for i, r in enumerate(subs):
        xa = sum(_dot(oh[i][h], wxo_ref[c, :]) for h, c in enumerate(heads))
        o_ref[0, r, :] = _layer_norm(alpha * x1[i] + xa, g2_ref[...], b2_ref[...])


def _post_mixer(x, moba_o, gla_o, w_mix_o, ln1_g, ln1_b, w_xq, kmem, vmem, w_xo, ln2_g, ln2_b, alpha, tm=1024):
    B, S, D = x.shape
    aw = moba_o.shape[-1]
    M = kmem.shape[1]
    wm1 = w_mix_o[:aw].astype(BF16)
    wm2 = w_mix_o[aw:].astype(BF16)
    wxq = w_xq.astype(BF16)
    wxo = w_xo.astype(BF16)
    vec = lambda a: a.reshape(1, D)
    row = lambda w: pl.BlockSpec((1, tm, w), lambda b, s: (b, s, 0))
    full = lambda a: pl.BlockSpec(a.shape, lambda b, s: (0,) * a.ndim)
    vspec = pl.BlockSpec((1, D), lambda b, s: (0, 0))
    memspec = pl.BlockSpec((1, M, D), lambda b, s: (b, 0, 0))
    return pl.pallas_call(
        functools.partial(_post_mixer_kernel, alpha),
        grid=(B, S // tm),
        in_specs=[row(D), row(aw), row(gla_o.shape[-1]), full(wm1), full(wm2), vspec, vspec, full(wxq),
                  memspec, memspec, full(wxo), vspec, vspec],
        out_specs=row(D),
        out_shape=jax.ShapeDtypeStruct((B, S, D), F32),
        compiler_params=pltpu.CompilerParams(
            dimension_semantics=("arbitrary", "arbitrary"), vmem_limit_bytes=VMEM_LIMIT_BYTES),
        name="post_mixer",
    )(x, moba_o, gla_o, wm1, wm2, vec(ln1_g), vec(ln1_b), wxq, kmem, vmem, wxo, vec(ln2_g), vec(ln2_b))


def _mlp_kernel(alpha, fc, x_ref, w1_ref, w2_ref, g_ref, b_ref, o_ref):
    for r in range(x_ref.shape[1] // MLP_SUB_ROWS):
        rows = slice(r * MLP_SUB_ROWS, (r + 1) * MLP_SUB_ROWS)
        x = x_ref[0, rows, :]
        xb = x.astype(BF16)
        acc = jnp.zeros_like(x)
        for c in range(w1_ref.shape[1] // fc):
            cols = slice(c * fc, (c + 1) * fc)
            h = jnp.maximum(_dot(xb, w1_ref[:, cols]), 0.0)
            acc = acc + _dot((h * h).astype(BF16), w2_ref[cols, :])
        o_ref[0, rows, :] = _layer_norm(alpha * x + acc, g_ref[...], b_ref[...])


def _mlp(x, w_ff1, w_ff2, g, b, alpha, tm=1024, fc=512):
    B, S, D = x.shape
    w1 = w_ff1.astype(BF16)
    w2 = w_ff2.astype(BF16)
    row = pl.BlockSpec((1, tm, D), lambda bb, s: (bb, s, 0))
    full = lambda a: pl.BlockSpec(a.shape, lambda bb, s: (0, 0), pipeline_mode=pl.Buffered(1))
    vspec = pl.BlockSpec((1, D), lambda bb, s: (0, 0))
    return pl.pallas_call(
        functools.partial(_mlp_kernel, alpha, fc),
        grid=(B, S // tm),
        in_specs=[row, full(w1), full(w2), vspec, vspec],
        out_specs=row,
        out_shape=jax.ShapeDtypeStruct((B, S, D), F32),
        compiler_params=pltpu.CompilerParams(
            dimension_semantics=("arbitrary", "arbitrary"), vmem_limit_bytes=VMEM_LIMIT_BYTES),
        name="sq_relu_mlp",
    )(x, w1, w2, g.reshape(1, D), b.reshape(1, D))


def kernel(x, mem, w_in, w_gate_up, b_gate, gla_norm_g, w_mix_o, ln1_g, ln1_b, w_xq, w_xkv, w_xo,
           ln2_g, ln2_b, w_ff1, w_ff2, ln3_g, ln3_b):
    depth = w_in.shape[0]
    alpha = (2.0 * depth) ** 0.25
    for l in range(depth):
        qa, ka, vat, qg, kg, vg, rg, la = _in_projection(x, w_in[l], w_gate_up[l], b_gate[l])
        moba_o = _moba(qa, ka, vat)
        gla_o = _gla(qg, kg, la, vg, rg, gla_norm_g[l].reshape(1, GLA_VAL_DIM))
        kmem, vmem = _mem_kv(mem, w_xkv[l])
        x = _post_mixer(x, moba_o, gla_o, w_mix_o[l], ln1_g[l], ln1_b[l], w_xq[l], kmem, vmem, w_xo[l],
                        ln2_g[l], ln2_b[l], alpha)
        x = _mlp(x, w_ff1[l], w_ff2[l], ln3_g[l], ln3_b[l], alpha)
    return x
```

```python
import functools

import jax
import jax.numpy as jnp
from jax import lax
from jax.experimental import pallas as pl
from jax.experimental.pallas import tpu as pltpu

F32 = jnp.float32
BF16 = jnp.bfloat16

MOBA_HEAD_DIM = 64
MOBA_VALUE_ROWS = 80
MOBA_BLOCK = 256
MOBA_TOPK = 3
GLA_HEADS = 4
GLA_KEY_DIM = 64
GLA_VAL_DIM = 128
GLA_GATE_RANK = 16
GLA_GATE_TEMP = 16.0
GLA_CHUNK = 64
GLA_GROUP = 256
XATTN_HEADS = 4
POST_SUB_ROWS = 256
WEIGHT_CAST_TILE = 256
MLP_SUB_ROWS = 512
LN_EPS = 1e-5
RMS_EPS = 1e-6

LANES = 128
VMEM_LIMIT_BYTES = 56 * 1024 * 1024

NEG_INF = float("-inf")
LOG2E = 1.4426950408889634


def _dot(a, b):
    return jnp.dot(a, b, preferred_element_type=F32)


def _dot_nt(a, b):
    return lax.dot_general(a, b, (((1,), (1,)), ((), ())), preferred_element_type=F32)


def _dot_tn(a, b):
    return lax.dot_general(a, b, (((0,), (0,)), ((), ())), preferred_element_type=F32)


def _split3(a):
    hi = a.astype(BF16)
    r1 = a - hi.astype(F32)
    mid = r1.astype(BF16)
    lo = (r1 - mid.astype(F32)).astype(BF16)
    return hi, mid, lo


def _layer_norm(x, g, b):
    mu = jnp.mean(x, axis=-1, keepdims=True)
    xc = x - mu
    var = jnp.mean(xc * xc, axis=-1, keepdims=True)
    return xc * lax.rsqrt(var + LN_EPS) * g + b


def _log_sigmoid(z):
    return jnp.minimum(z, 0.0) - jnp.log(1.0 + jnp.exp(-jnp.abs(z)))


def _sigmoid(z):
    return 1.0 / (1.0 + jnp.exp(-z))


def _inproj_kernel(x_ref, w_ref, wglr_ref, wz_ref, bg_ref,
                   qa_ref, ka_ref, vat_ref, qg_ref, kg_ref, vg_ref, rg_ref, la_ref,
                   wqk_ref, wvat_ref, wg_ref, wvr_ref):
    aw, kw, vw = qa_ref.shape[-1], qg_ref.shape[-1], vg_ref.shape[-1]

    @pl.when((pl.program_id(0) == 0) & (pl.program_id(1) == 0))
    def _():
        t = WEIGHT_CAST_TILE
        for r in range(0, w_ref.shape[0], t):
            rows = slice(r, r + t)
            wqk_ref[rows, :] = w_ref[rows, :2 * aw].astype(BF16)
            for c in range(0, aw, t):
                wvat_ref[c:c + t, rows] = w_ref[rows, 2 * aw + c:2 * aw + c + t].T.astype(BF16)
            wg_ref[rows, :2 * kw] = w_ref[rows, 3 * aw:3 * aw + 2 * kw].astype(BF16)
            wg_ref[rows, 2 * kw:] = wglr_ref[rows, :]
            wvr_ref[rows, :] = w_ref[rows, 3 * aw + 2 * kw:3 * aw + 2 * kw + 2 * vw].astype(BF16)

    xb = x_ref[0].astype(BF16)
    qk = _dot(xb, wqk_ref[...])
    qa_ref[0] = (qk[:, :aw] * (MOBA_HEAD_DIM ** -0.5 * LOG2E)).astype(BF16)
    ka_ref[0] = qk[:, aw:].astype(BF16)
    vt = _dot_nt(wvat_ref[...], xb).astype(BF16)
    hd, va = MOBA_HEAD_DIM, MOBA_VALUE_ROWS
    for h in range(aw // hd):
        vat_ref[0, h * va:h * va + hd, :] = vt[h * hd:(h + 1) * hd, :]
        vat_ref[0, h * va + hd:(h + 1) * va, :] = jnp.ones((va - hd, vt.shape[1]), BF16)
    g = _dot(xb, wg_ref[...])
    qg_ref[0] = g[:, :kw] * (GLA_KEY_DIM ** -0.5)
    kg_ref[0] = g[:, kw:2 * kw]
    vr = _dot(xb, wvr_ref[...])
    vg_ref[0] = vr[:, :vw].astype(BF16)
    rg = vr[:, vw:]
    rg_ref[0] = rg * _sigmoid(rg)
    gh, gm, gl = _split3(g[:, 2 * kw:])
    grp = lax.broadcasted_iota(jnp.int32, (1, LANES), 1) // GLA_GATE_RANK
    lhs = jnp.where((grp == 1) | (grp == 4), gm, jnp.where(grp == 2, gl, gh))
    z = _dot(lhs, wz_ref[...]) + bg_ref[...]
    la_ref[0] = _log_sigmoid(z) / GLA_GATE_TEMP


def _in_projection(x, w_in, w_gate_up, b_gate, tm=1024):
    B, S, D = x.shape
    aw = 512
    vrows = aw // MOBA_HEAD_DIM * MOBA_VALUE_ROWS
    kw = GLA_HEADS * GLA_KEY_DIM
    vw = GLA_HEADS * GLA_VAL_DIM
    wglr = w_in[:, 3 * aw + 2 * kw + 2 * vw:]
    n_terms = 6
    pad = LANES - n_terms * GLA_GATE_RANK
    wglr_rep = jnp.pad(jnp.tile(wglr, (1, n_terms)), ((0, 0), (0, pad))).astype(BF16)
    wh, wm, wl = _split3(w_gate_up)
    wz = jnp.pad(jnp.concatenate([wh, wh, wh, wm, wm, wl], axis=0), ((0, pad), (0, 0)))
    bg = b_gate.reshape(1, kw)

    def full(a):
        return pl.BlockSpec(a.shape, lambda b, s: (0,) * a.ndim)

    row = lambda w: pl.BlockSpec((1, tm, w), lambda b, s: (b, s, 0))
    out_shape = (
        jax.ShapeDtypeStruct((B, S, aw), BF16),
        jax.ShapeDtypeStruct((B, S, aw), BF16),
        jax.ShapeDtypeStruct((B, vrows, S), BF16),
        jax.ShapeDtypeStruct((B, S, kw), F32),
        jax.ShapeDtypeStruct((B, S, kw), F32),
        jax.ShapeDtypeStruct((B, S, vw), BF16),
        jax.ShapeDtypeStruct((B, S, vw), F32),
        jax.ShapeDtypeStruct((B, S, kw), F32),
    )
    out_specs = (row(aw), row(aw), pl.BlockSpec((1, vrows, tm), lambda b, s: (b, 0, s)),
                 row(kw), row(kw), row(vw), row(vw), row(kw))
    return pl.pallas_call(
        _inproj_kernel,
        grid=(B, S // tm),
        in_specs=[row(D), pl.BlockSpec(w_in.shape, lambda b, s: (0, 0), pipeline_mode=pl.Buffered(1)),
                  full(wglr_rep), full(wz), full(bg)],
        out_specs=out_specs,
        out_shape=out_shape,
        scratch_shapes=[
            pltpu.VMEM((D, 2 * aw), BF16),
            pltpu.VMEM((aw, D), BF16),
            pltpu.VMEM((D, 2 * kw + LANES), BF16),
            pltpu.VMEM((D, 2 * vw), BF16),
        ],
        compiler_params=pltpu.CompilerParams(
            dimension_semantics=("arbitrary", "arbitrary"), vmem_limit_bytes=VMEM_LIMIT_BYTES),
        name="in_projection",
    )(x, w_in, wglr_rep, wz, bg)


def _alibi_tables(n_heads, bs):
    slopes = 2.0 ** (-8.0 * jnp.arange(1, n_heads + 1, dtype=F32) / n_heads)
    sig = slopes * LOG2E
    rel = jnp.arange(bs, dtype=F32)
    a = -sig[:, None] * rel[None, :]
    sig_b = jnp.broadcast_to(sig[:, None], a.shape)
    cols = jnp.stack(list(_split3(a)) + list(_split3(sig_b)), axis=-1)
    qaux = jnp.concatenate([cols[0::2], cols[1::2]], axis=-1)
    qaux = jnp.pad(qaux, ((0, 0), (0, 0), (0, LANES - qaux.shape[-1])))
    one = jnp.ones((bs,), F32)
    kcols = jnp.stack([one, one, one, rel, rel, rel], axis=-1)
    zeros = jnp.zeros_like(kcols)
    kaux = jnp.stack([jnp.concatenate([kcols, zeros], axis=-1), jnp.concatenate([zeros, kcols], axis=-1)])
    kaux = jnp.pad(kaux, ((0, 0), (0, 0), (0, LANES - kaux.shape[-1]))).astype(BF16)
    return sig, qaux, kaux


def _moba_kernel(n_cast, sig_ref, q_ref, k_ref, vt_ref, qaux_ref, kaux_ref, *refs):
    cast_in, (o_ref, *cast_out), scratch = refs[:n_cast], refs[n_cast:2 * n_cast + 1], refs[2 * n_cast + 1:]
    kmean_ref, kaug_ref, kms_ref, sel_ref, acc_ref, m_ref, s_ref, mx_ref = scratch
    for src, dst in zip(cast_in, cast_out):
        dst[...] = src[...].astype(BF16)
    hg = pl.program_id(1)
    i = pl.program_id(2)
    bs = MOBA_BLOCK
    nb = k_ref.shape[1] // bs
    hd = MOBA_HEAD_DIM
    va = MOBA_VALUE_ROWS
    n_heads = q_ref.shape[-1] // hd
    n_pairs = n_heads // 2
    pair_lanes = [slice(p * LANES, (p + 1) * LANES) for p in range(n_pairs)]
    gate_rows = 3 * nb

    @pl.when(i == 0)
    def _():
        lane = lax.broadcasted_iota(jnp.int32, (1, LANES), 1)
        first = lane < hd
        for n in range(nb):
            kb = k_ref[0, n * bs:(n + 1) * bs, :]
            kmean_ref[n:n + 1, :] = jnp.sum(kb.astype(F32), axis=0, keepdims=True) * (1.0 / bs)
            for p in range(n_pairs):
                kp = kb[:, pair_lanes[p]]
                zero = jnp.zeros_like(kp)
                for h, kh in enumerate((jnp.where(first, kp, zero), jnp.where(first, zero, kp))):
                    rows = slice((2 * n + h) * bs, (2 * n + h + 1) * bs)
                    kaug_ref[p, rows, :] = jnp.concatenate([kh, kaux_ref[h]], axis=1)
        pieces = _split3(kmean_ref[...])
        for p in range(n_pairs):
            for h in range(2):
                for r, piece in enumerate(pieces):
                    pp = piece[:, pair_lanes[p]]
                    zero = jnp.zeros_like(pp)
                    ph = jnp.where(first, pp, zero) if h == 0 else jnp.where(first, zero, pp)
                    rows = slice(h * gate_rows + r * nb, h * gate_rows + (r + 1) * nb)
                    kms_ref[p, rows, :] = jnp.concatenate([ph, zero], axis=1)

    qaug = [jnp.concatenate([q_ref[0, :, pair_lanes[p]], qaux_ref[p]], axis=1) for p in range(n_pairs)]
    sig = [sig_ref[hg * n_heads + h] for h in range(n_heads)]

    def select_blocks(h, g):
        nidx = lax.broadcasted_iota(jnp.int32, (nb, bs), 0)
        g = jnp.where(nidx < i, g, NEG_INF)
        sel = jnp.zeros((nb, bs), jnp.bool_)
        for _ in range(MOBA_TOPK):
            best = jnp.max(g, axis=0, keepdims=True)
            first = jnp.min(jnp.where(g == best, nidx, nb), axis=0, keepdims=True)
            pick = nidx == first
            sel = sel | pick
            g = jnp.where(pick, NEG_INF, g)
        sel_ref[h] = jnp.where(sel & (nidx < i), 0.0, NEG_INF)

    def stage_a(blk, slot, p, with_gate=False):
        start = pl.multiple_of(blk * (2 * bs), 2 * bs)
        lhs = kaug_ref[p, pl.ds(start, 2 * bs), :]
        if with_gate:
            lhs = jnp.concatenate([lhs, kms_ref[p]], axis=0)
        t = _dot_nt(lhs, qaug[p])
        for h in range(2):
            if with_gate:
                g0 = 2 * bs + h * gate_rows
                select_blocks(2 * p + h, t[g0:g0 + nb] + t[g0 + nb:g0 + 2 * nb] + t[g0 + 2 * nb:g0 + 3 * nb])
            th = t[h * bs:(h + 1) * bs]
            s_ref[slot, 2 * p + h] = th
            mx_ref[slot, 2 * p + h:2 * p + h + 1, :] = jnp.max(th, axis=0, keepdims=True)

    def values(blk, h):
        start = pl.multiple_of(blk * bs, bs)
        return vt_ref[0, h * va:(h + 1) * va, pl.ds(start, bs)]

    def stage_b_own(slot, p):
        krow = lax.broadcasted_iota(jnp.int32, (bs, bs), 0)
        qcol = lax.broadcasted_iota(jnp.int32, (bs, bs), 1)
        causal = qcol >= krow
        for h in (2 * p, 2 * p + 1):
            t = jnp.where(causal, s_ref[slot, h], NEG_INF)
            m = jnp.max(t, axis=0, keepdims=True)
            pr = jnp.exp2(t - m)
            m_ref[h:h + 1, :] = m
            acc_ref[h] = _dot(values(i, h), pr.astype(BF16))

    def stage_b_past(blk, slot, p):
        off = ((i - blk) * bs).astype(F32)
        for h in (2 * p, 2 * p + 1):
            cs = sel_ref[h, pl.ds(blk, 1), :] - sig[h] * off
            m_old = m_ref[h:h + 1, :]
            m_new = jnp.maximum(m_old, mx_ref[slot, h:h + 1, :] + cs)
            alpha = jnp.exp2(m_old - m_new)
            pr = jnp.exp2(s_ref[slot, h] + (cs - m_new))
            m_ref[h:h + 1, :] = m_new
            acc_ref[h] = alpha * acc_ref[h] + _dot(values(blk, h), pr.astype(BF16))

    for p in range(n_pairs):
        stage_a(i, 0, p, with_gate=True)
    for p in range(n_pairs):
        stage_a(jnp.maximum(i - 1, 0), 1, p)
        stage_b_own(0, p)

    def step(u, carry):
        blk = i - 1 - 2 * u
        for p in range(n_pairs):
            stage_a(blk - 1, 0, p)
            stage_b_past(blk, 1, p)
        for p in range(n_pairs):
            stage_a(jnp.maximum(blk - 2, 0), 1, p)
            stage_b_past(blk - 1, 0, p)
        return carry

    lax.fori_loop(0, i // 2, step, 0)

    @pl.when(i % 2 == 1)
    def _():
        for p in range(n_pairs):
            stage_b_past(0, 1, p)

    for p in range(n_pairs):
        ot = jnp.concatenate([acc_ref[2 * p + h, :hd, :] / acc_ref[2 * p + h, hd:hd + 1, :] for h in range(2)], axis=0)
        o_ref[0, :, pair_lanes[p]] = ot.T.astype(BF16)


def _moba(qa, ka, vat, cast_weights=(), heads_per_step=8):
    B, S, W = qa.shape
    bs = MOBA_BLOCK
    nb = S // bs
    n_heads = W // MOBA_HEAD_DIM
    gw = heads_per_step * MOBA_HEAD_DIM
    sig, qaux, kaux = _alibi_tables(n_heads, bs)
    n_steps = B * (W // gw) * nb
    cast_specs = [pl.BlockSpec((w.shape[0] // n_steps, w.shape[1]), lambda b, hg, i: ((b * (W // gw) + hg) * nb + i, 0))
                  for w in cast_weights]
    outs = pl.pallas_call(
        functools.partial(_moba_kernel, len(cast_weights)),
        grid=(B, W // gw, nb),
        in_specs=[
            pl.BlockSpec(memory_space=pltpu.SMEM),
            pl.BlockSpec((1, bs, gw), lambda b, hg, i: (b, i, hg)),
            pl.BlockSpec((1, S, gw), lambda b, hg, i: (b, 0, hg)),
            pl.BlockSpec((1, heads_per_step * MOBA_VALUE_ROWS, S), lambda b, hg, i: (b, hg, 0)),
            pl.BlockSpec((heads_per_step // 2, bs, LANES), lambda b, hg, i: (hg, 0, 0)),
            pl.BlockSpec((2, bs, LANES), lambda b, hg, i: (0, 0, 0)),
        ] + cast_specs,
        out_specs=[pl.BlockSpec((1, bs, gw), lambda b, hg, i: (b, i, hg))] + cast_specs,
        out_shape=[jax.ShapeDtypeStruct((B, S, W), BF16)] + [jax.ShapeDtypeStruct(w.shape, BF16) for w in cast_weights],
        scratch_shapes=[
            pltpu.VMEM((nb, gw), F32),
            pltpu.VMEM((heads_per_step // 2, 2 * S, 2 * LANES), BF16),
            pltpu.VMEM((heads_per_step // 2, 6 * nb, 2 * LANES), BF16),
            pltpu.VMEM((heads_per_step, nb, bs), F32),
            pltpu.VMEM((heads_per_step, MOBA_VALUE_ROWS, bs), F32),
            pltpu.VMEM((8, bs), F32),
            pltpu.VMEM((2, heads_per_step, bs, bs), F32),
            pltpu.VMEM((2, 8, bs), F32),
        ],
        compiler_params=pltpu.CompilerParams(
            dimension_semantics=("arbitrary", "arbitrary", "arbitrary"), vmem_limit_bytes=VMEM_LIMIT_BYTES),
        name="moba_attention",
    )(sig, qa, ka, vat, qaux, kaux, *cast_weights)
    return outs[0], outs[1:]


def _gla_kernel(q_ref, k_ref, la_ref, v_ref, r_ref, gn_ref, o_ref, st_ref):
    s_idx = pl.program_id(1)
    C = GLA_CHUNK
    tc = q_ref.shape[1]

    @pl.when(s_idx == 0)
    def _():
        st_ref[...] = jnp.zeros_like(st_ref)

    G = GLA_GROUP
    n_pairs = GLA_HEADS // 2
    lane = lax.broadcasted_iota(jnp.int32, (1, LANES), 1)
    head_lanes = (lane < GLA_KEY_DIM, lane >= GLA_KEY_DIM)
    ti = lax.broadcasted_iota(jnp.int32, (G, G), 0)
    si = lax.broadcasted_iota(jnp.int32, (G, G), 1)
    causal = (ti // C == si // C) & (si <= ti)
    tril = jnp.where(causal, 1.0, 0.0).astype(BF16)
    gn = gn_ref[...]

    cpg = G // C
    n_groups = tc // G
    grp_rows = [slice(g * G, (g + 1) * G) for g in range(n_groups)]
    chunk_rows = [slice(c * C, (c + 1) * C) for c in range(cpg)]
    pair_lanes = [slice(hp * LANES, (hp + 1) * LANES) for hp in range(n_pairs)]
    val_lanes = [slice(hd * GLA_VAL_DIM, (hd + 1) * GLA_VAL_DIM) for hd in range(GLA_HEADS)]

    b = [sum(_dot(tril, piece) for piece in _split3(la_ref[0, rows, :])) for rows in grp_rows]

    q_m, k_inv, k_dec, decay = [], [], [], []
    for g, rows in enumerate(grp_rows):
        b_last = [b[g][c * C + C - 1:c * C + C, :] for c in range(cpg)]
        b_end = jnp.concatenate([jnp.broadcast_to(bl, (C, bl.shape[1])) for bl in b_last], axis=0)
        q_dec = q_ref[0, rows, :] * jnp.exp(b[g])
        ks = k_ref[0, rows, :]
        k_inv.append((ks * jnp.exp(-b[g])).astype(BF16))
        k_dec.append((ks * jnp.exp(b_end - b[g])).astype(BF16))
        decay.append([jnp.exp(bl) for bl in b_last])
        q_m.append([jnp.where(head_lanes[hd % 2], q_dec[:, pair_lanes[hd // 2]], 0.0).astype(BF16)
                    for hd in range(GLA_HEADS)])

    attn = [[_dot_nt(q_m[g][hd], k_inv[g][:, pair_lanes[hd // 2]]) for hd in range(GLA_HEADS)]
            for g in range(n_groups)]

    update = [[[None] * cpg for _ in range(n_pairs)] for _ in range(n_groups)]
    for g, rows in enumerate(grp_rows):
        for hp in range(n_pairs):
            v_pair = v_ref[0, rows, hp * 2 * GLA_VAL_DIM:(hp + 1) * 2 * GLA_VAL_DIM]
            for c, cr in enumerate(chunk_rows):
                u = _dot_tn(v_pair[cr], k_dec[g][cr, pair_lanes[hp]])
                update[g][hp][c] = jnp.where(head_lanes[0], u[:GLA_VAL_DIM], u[GLA_VAL_DIM:])

    o_intra = [[_dot(jnp.where(causal, attn[g][hd], 0.0).astype(BF16), v_ref[0, grp_rows[g], val_lanes[hd]])
                for hd in range(GLA_HEADS)] for g in range(n_groups)]

    states = [st_ref[hp] for hp in range(n_pairs)]
    for g, rows in enumerate(grp_rows):
        o_inter = [[None] * cpg for _ in range(GLA_HEADS)]
        for c, cr in enumerate(chunk_rows):
            for hp in range(n_pairs):
                q_pair = jnp.concatenate([q_m[g][2 * hp][cr], q_m[g][2 * hp + 1][cr]], axis=0)
                oi = _dot_nt(q_pair, states[hp].astype(BF16))
                o_inter[2 * hp][c] = oi[:C]
                o_inter[2 * hp + 1][c] = oi[C:]
                states[hp] = states[hp] * decay[g][c][:, pair_lanes[hp]] + update[g][hp][c]
        for hd in range(GLA_HEADS):
            o = jnp.concatenate(o_inter[hd], axis=0) + o_intra[g][hd]
            ms = jnp.mean(o * o, axis=-1, keepdims=True)
            o_ref[0, rows, val_lanes[hd]] = (o * lax.rsqrt(ms + RMS_EPS) * gn * r_ref[0, rows, val_lanes[hd]]).astype(BF16)

    for hp in range(n_pairs):
        st_ref[hp] = states[hp]


def _gla(qg, kg, la, vg, rg, gnorm, tc=512):
    B, S, kw = qg.shape
    vw = vg.shape[-1]
    row = lambda w: pl.BlockSpec((1, tc, w), lambda b, s: (b, s, 0))
    return pl.pallas_call(
        _gla_kernel,
        grid=(B, S // tc),
        in_specs=[row(kw), row(kw), row(kw), row(vw), row(vw),
                  pl.BlockSpec((1, GLA_VAL_DIM), lambda b, s: (0, 0))],
        out_specs=row(vw),
        out_shape=jax.ShapeDtypeStruct((B, S, vw), BF16),
        scratch_shapes=[pltpu.VMEM((GLA_HEADS // 2, GLA_VAL_DIM, LANES), F32)],
        compiler_params=pltpu.CompilerParams(
            dimension_semantics=("arbitrary", "arbitrary"), vmem_limit_bytes=VMEM_LIMIT_BYTES),
        name="gla_scan",
    )(qg, kg, la, vg, rg, gnorm)


def _memkv_kernel(mem_ref, w_ref, k_ref, v_ref):
    kv = _dot(mem_ref[0].astype(BF16), w_ref[...])
    d = k_ref.shape[-1]
    k_ref[0] = kv[:, :d].astype(BF16)
    v_ref[0] = kv[:, d:].astype(BF16)


def _mem_kv(mem, w_xkv):
    B, M, D = mem.shape
    wb = w_xkv.astype(BF16)
    blk = pl.BlockSpec((1, M, D), lambda b: (b, 0, 0))
    return pl.pallas_call(
        _memkv_kernel,
        grid=(B,),
        in_specs=[blk, pl.BlockSpec(wb.shape, lambda b: (0, 0))],
        out_specs=(blk, blk),
        out_shape=(jax.ShapeDtypeStruct((B, M, D), BF16), jax.ShapeDtypeStruct((B, M, D), BF16)),
        compiler_params=pltpu.CompilerParams(
            dimension_semantics=("arbitrary",), vmem_limit_bytes=VMEM_LIMIT_BYTES),
        name="mem_kv_projection",
    )(mem, wb)


def _post_mixer_kernel(alpha, x_ref, mo_ref, go_ref, wm_ref, g1_ref, b1_ref, wxq_ref,
                       km_ref, vm_ref, wxo_ref, g2_ref, b2_ref, o_ref):
    tm, d = x_ref.shape[1], x_ref.shape[2]
    hd = d // XATTN_HEADS
    subs = [slice(r * POST_SUB_ROWS, (r + 1) * POST_SUB_ROWS) for r in range(tm // POST_SUB_ROWS)]
    heads = [slice(h * hd, (h + 1) * hd) for h in range(XATTN_HEADS)]
    aw = mo_ref.shape[-1]
    mix = [_dot(mo_ref[0, r, :], wm_ref[:aw, :]) + _dot(go_ref[0, r, :], wm_ref[aw:, :]) for r in subs]
    x1 = [_layer_norm(alpha * x_ref[0, r, :] + mix[i], g1_ref[...], b1_ref[...]) for i, r in enumerate(subs)]
    q = [(_dot(x1[i].astype(BF16), wxq_ref[...]) * (hd ** -0.5)).astype(BF16) for i in range(len(subs))]
    s = [[_dot_nt(q[i][:, c], km_ref[0, :, c]) for c in heads] for i in range(len(subs))]
    oh = []
    for i in range(len(subs)):
        row = []
        for h, c in enumerate(heads):
            m = jnp.max(s[i][h], axis=-1, keepdims=True)
            p = jnp.exp(s[i][h] - m)
            l = jnp.sum(p, axis=-1, keepdims=True)
            row.append((_dot(p.astype(BF16), vm_ref[0, :, c]) / l).astype(BF16))
        oh.append(row)
    for i, r in enumerate(subs):
        xa = sum(_dot(oh[i][h], wxo_ref[c, :]) for h, c in enumerate(heads))
        o_ref[0, r, :] = _layer_norm(alpha * x1[i] + xa, g2_ref[...], b2_ref[...])


def _post_mixer(x, moba_o, gla_o, w_mix_o, ln1_g, ln1_b, w_xq, kmem, vmem, w_xo, ln2_g, ln2_b, alpha, tm=1024):
    B, S, D = x.shape
    aw = moba_o.shape[-1]
    M = kmem.shape[1]
    wm = w_mix_o.astype(BF16)
    wxq = w_xq.astype(BF16)
    wxo = w_xo.astype(BF16)
    vec = lambda a: a.reshape(1, D)
    row = lambda w: pl.BlockSpec((1, tm, w), lambda b, s: (b, s, 0))
    full = lambda a: pl.BlockSpec(a.shape, lambda b, s: (0,) * a.ndim)
    vspec = pl.BlockSpec((1, D), lambda b, s: (0, 0))
    memspec = pl.BlockSpec((1, M, D), lambda b, s: (b, 0, 0))
    return pl.pallas_call(
        functools.partial(_post_mixer_kernel, alpha),
        grid=(B, S // tm),
        in_specs=[row(D), row(aw), row(gla_o.shape[-1]), full(wm), vspec, vspec, full(wxq),
                  memspec, memspec, full(wxo), vspec, vspec],
        out_specs=row(D),
        out_shape=jax.ShapeDtypeStruct((B, S, D), F32),
        compiler_params=pltpu.CompilerParams(
            dimension_semantics=("arbitrary", "arbitrary"), vmem_limit_bytes=VMEM_LIMIT_BYTES),
        name="post_mixer",
    )(x, moba_o, gla_o, wm, vec(ln1_g), vec(ln1_b), wxq, kmem, vmem, wxo, vec(ln2_g), vec(ln2_b))


def _mlp_kernel(alpha, fc, x_ref, w1_ref, w2_ref, g_ref, b_ref, o_ref):
    for r in range(x_ref.shape[1] // MLP_SUB_ROWS):
        rows = slice(r * MLP_SUB_ROWS, (r + 1) * MLP_SUB_ROWS)
        x = x_ref[0, rows, :]
        xb = x.astype(BF16)
        acc = jnp.zeros_like(x)
        for c in range(w1_ref.shape[1] // fc):
            cols = slice(c * fc, (c + 1) * fc)
            h = jnp.maximum(_dot(xb, w1_ref[:, cols]), 0.0)
            acc = acc + _dot((h * h).astype(BF16), w2_ref[cols, :])
        o_ref[0, rows, :] = _layer_norm(alpha * x + acc, g_ref[...], b_ref[...])


def _mlp(x, w_ff1, w_ff2, g, b, alpha, tm=1024, fc=512):
    B, S, D = x.shape
    w1 = w_ff1.astype(BF16)
    w2 = w_ff2.astype(BF16)
    row = pl.BlockSpec((1, tm, D), lambda bb, s: (bb, s, 0))
    full = lambda a: pl.BlockSpec(a.shape, lambda bb, s: (0, 0), pipeline_mode=pl.Buffered(1))
    vspec = pl.BlockSpec((1, D), lambda bb, s: (0, 0))
    return pl.pallas_call(
        functools.partial(_mlp_kernel, alpha, fc),
        grid=(B, S // tm),
        in_specs=[row, full(w1), full(w2), vspec, vspec],
        out_specs=row,
        out_shape=jax.ShapeDtypeStruct((B, S, D), F32),
        compiler_params=pltpu.CompilerParams(
            dimension_semantics=("arbitrary", "arbitrary"), vmem_limit_bytes=VMEM_LIMIT_BYTES),
        name="sq_relu_mlp",
    )(x, w1, w2, g.reshape(1, D), b.reshape(1, D))


def kernel(x, mem, w_in, w_gate_up, b_gate, gla_norm_g, w_mix_o, ln1_g, ln1_b, w_xq, w_xkv, w_xo,
           ln2_g, ln2_b, w_ff1, w_ff2, ln3_g, ln3_b):
    depth = w_in.shape[0]
    alpha = (2.0 * depth) ** 0.25
    for l in range(depth):
        qa, ka, vat, qg, kg, vg, rg, la = _in_projection(x, w_in[l], w_gate_up[l], b_gate[l])
        moba_o, (wkv, wmix, wxq, wxo, wf1, wf2) = _moba(
            qa, ka, vat, cast_weights=(w_xkv[l], w_mix_o[l], w_xq[l], w_xo[l], w_ff1[l], w_ff2[l]))
        gla_o = _gla(qg, kg, la, vg, rg, gla_norm_g[l].reshape(1, GLA_VAL_DIM))
        kmem, vmem = _mem_kv(mem, wkv)
        x = _post_mixer(x, moba_o, gla_o, wmix, ln1_g[l], ln1_b[l], wxq, kmem, vmem, wxo, ln2_g[l], ln2_b[l], alpha)
        x = _mlp(x, wf1, wf2, ln3_g[l], ln3_b[l], alpha)
    return x
```

```python
import functools

import jax
import jax.numpy as jnp
from jax import lax
from jax.experimental import pallas as pl
from jax.experimental.pallas import tpu as pltpu

F32 = jnp.float32
BF16 = jnp.bfloat16

MOBA_HEAD_DIM = 64
MOBA_VALUE_ROWS = 80
MOBA_BLOCK = 256
MOBA_TOPK = 3
GLA_HEADS = 4
GLA_KEY_DIM = 64
GLA_VAL_DIM = 128
GLA_GATE_RANK = 16
GLA_GATE_TEMP = 16.0
GATE_TERMS = 6
GLA_CHUNK = 64
GLA_GROUP = 256
XATTN_HEADS = 4
POST_SUB_ROWS = 256
WEIGHT_CAST_TILE = 256
MLP_SUB_ROWS = 512
LN_EPS = 1e-5
RMS_EPS = 1e-6

LANES = 128
VMEM_LIMIT_BYTES = 56 * 1024 * 1024

NEG_INF = float("-inf")
LOG2E = 1.4426950408889634


def _dot(a, b):
    return jnp.dot(a, b, preferred_element_type=F32)


def _dot_nt(a, b):
    return lax.dot_general(a, b, (((1,), (1,)), ((), ())), preferred_element_type=F32)


def _dot_tn(a, b):
    return lax.dot_general(a, b, (((0,), (0,)), ((), ())), preferred_element_type=F32)


def _split3(a):
    hi = a.astype(BF16)
    r1 = a - hi.astype(F32)
    mid = r1.astype(BF16)
    lo = (r1 - mid.astype(F32)).astype(BF16)
    return hi, mid, lo


def _layer_norm(x, g, b):
    mu = jnp.mean(x, axis=-1, keepdims=True)
    xc = x - mu
    var = jnp.mean(xc * xc, axis=-1, keepdims=True)
    return xc * lax.rsqrt(var + LN_EPS) * g + b


def _log_sigmoid(z):
    return jnp.minimum(z, 0.0) - jnp.log(1.0 + jnp.exp(-jnp.abs(z)))


def _sigmoid(z):
    return 1.0 / (1.0 + jnp.exp(-z))


def _inproj_kernel(x_ref, wt_ref, wz_ref, bg_ref,
                   qa_ref, ka_ref, vat_ref, qg_ref, kg_ref, vg_ref, rg_ref, la_ref,
                   wqk_ref, wvat_ref, wg_ref, wvr_ref):
    aw, kw, vw = qa_ref.shape[-1], qg_ref.shape[-1], vg_ref.shape[-1]

    @pl.when((pl.program_id(0) == 0) & (pl.program_id(1) == 0))
    def _():
        def convert(dst, first_row):
            for r in range(0, dst.shape[0], WEIGHT_CAST_TILE):
                n = min(WEIGHT_CAST_TILE, dst.shape[0] - r)
                dst[r:r + n, :] = wt_ref[first_row + r:first_row + r + n, :].astype(BF16)

        convert(wqk_ref, 0)
        convert(wvat_ref, 2 * aw)
        convert(wg_ref.at[:2 * kw], 3 * aw)
        convert(wvr_ref, 3 * aw + 2 * kw)
        glr_row = 3 * aw + 2 * kw + 2 * vw
        glr = wt_ref[glr_row:glr_row + GLA_GATE_RANK, :].astype(BF16)
        for j in range(LANES // GLA_GATE_RANK):
            rows = slice(2 * kw + j * GLA_GATE_RANK, 2 * kw + (j + 1) * GLA_GATE_RANK)
            wg_ref[rows, :] = glr if j < GATE_TERMS else jnp.zeros_like(glr)

    xb = x_ref[0].astype(BF16)
    qk = _dot_nt(xb, wqk_ref[...])
    qa_ref[0] = (qk[:, :aw] * (MOBA_HEAD_DIM ** -0.5 * LOG2E)).astype(BF16)
    ka_ref[0] = qk[:, aw:].astype(BF16)
    vt = _dot_nt(wvat_ref[...], xb).astype(BF16)
    hd, va = MOBA_HEAD_DIM, MOBA_VALUE_ROWS
    for h in range(aw // hd):
        vat_ref[0, h * va:h * va + hd, :] = vt[h * hd:(h + 1) * hd, :]
        vat_ref[0, h * va + hd:(h + 1) * va, :] = jnp.ones((va - hd, vt.shape[1]), BF16)
    g = _dot_nt(xb, wg_ref[...])
    qg_ref[0] = g[:, :kw] * (GLA_KEY_DIM ** -0.5)
    kg_ref[0] = g[:, kw:2 * kw]
    vr = _dot_nt(xb, wvr_ref[...])
    vg_ref[0] = vr[:, :vw].astype(BF16)
    rg = vr[:, vw:]
    rg_ref[0] = rg * _sigmoid(rg)
    gh, gm, gl = _split3(g[:, 2 * kw:])
    grp = lax.broadcasted_iota(jnp.int32, (1, LANES), 1) // GLA_GATE_RANK
    lhs = jnp.where((grp == 1) | (grp == 4), gm, jnp.where(grp == 2, gl, gh))
    z = _dot(lhs, wz_ref[...]) + bg_ref[...]
    la_ref[0] = _log_sigmoid(z) / GLA_GATE_TEMP


def _in_projection(x, w_in, w_gate_up, b_gate, tm=1024):
    B, S, D = x.shape
    aw = 512
    vrows = aw // MOBA_HEAD_DIM * MOBA_VALUE_ROWS
    kw = GLA_HEADS * GLA_KEY_DIM
    vw = GLA_HEADS * GLA_VAL_DIM
    wt = jnp.swapaxes(w_in, 0, 1)
    wh, wm, wl = _split3(w_gate_up)
    pad = LANES - GATE_TERMS * GLA_GATE_RANK
    wz = jnp.pad(jnp.concatenate([wh, wh, wh, wm, wm, wl], axis=0), ((0, pad), (0, 0)))
    bg = b_gate.reshape(1, kw)

    def full(a):
        return pl.BlockSpec(a.shape, lambda b, s: (0,) * a.ndim)

    row = lambda w: pl.BlockSpec((1, tm, w), lambda b, s: (b, s, 0))
    out_shape = (
        jax.ShapeDtypeStruct((B, S, aw), BF16),
        jax.ShapeDtypeStruct((B, S, aw), BF16),
        jax.ShapeDtypeStruct((B, vrows, S), BF16),
        jax.ShapeDtypeStruct((B, S, kw), F32),
        jax.ShapeDtypeStruct((B, S, kw), F32),
        jax.ShapeDtypeStruct((B, S, vw), BF16),
        jax.ShapeDtypeStruct((B, S, vw), F32),
        jax.ShapeDtypeStruct((B, S, kw), F32),
    )
    out_specs = (row(aw), row(aw), pl.BlockSpec((1, vrows, tm), lambda b, s: (b, 0, s)),
                 row(kw), row(kw), row(vw), row(vw), row(kw))
    return pl.pallas_call(
        _inproj_kernel,
        grid=(B, S // tm),
        in_specs=[row(D), pl.BlockSpec(wt.shape, lambda b, s: (0, 0), pipeline_mode=pl.Buffered(1)),
                  full(wz), full(bg)],
        out_specs=out_specs,
        out_shape=out_shape,
        scratch_shapes=[
            pltpu.VMEM((2 * aw, D), BF16),
            pltpu.VMEM((aw, D), BF16),
            pltpu.VMEM((2 * kw + LANES, D), BF16),
            pltpu.VMEM((2 * vw, D), BF16),
        ],
        compiler_params=pltpu.CompilerParams(
            dimension_semantics=("arbitrary", "arbitrary"), vmem_limit_bytes=VMEM_LIMIT_BYTES),
        name="in_projection",
    )(x, wt, wz, bg)


def _alibi_tables(n_heads, bs):
    slopes = 2.0 ** (-8.0 * jnp.arange(1, n_heads + 1, dtype=F32) / n_heads)
    sig = slopes * LOG2E
    rel = jnp.arange(bs, dtype=F32)
    a = -sig[:, None] * rel[None, :]
    sig_b = jnp.broadcast_to(sig[:, None], a.shape)
    cols = jnp.stack(list(_split3(a)) + list(_split3(sig_b)), axis=-1)
    qaux = jnp.concatenate([cols[0::2], cols[1::2]], axis=-1)
    qaux = jnp.pad(qaux, ((0, 0), (0, 0), (0, LANES - qaux.shape[-1])))
    one = jnp.ones((bs,), F32)
    kcols = jnp.stack([one, one, one, rel, rel, rel], axis=-1)
    zeros = jnp.zeros_like(kcols)
    kaux = jnp.stack([jnp.concatenate([kcols, zeros], axis=-1), jnp.concatenate([zeros, kcols], axis=-1)])
    kaux = jnp.pad(kaux, ((0, 0), (0, 0), (0, LANES - kaux.shape[-1]))).astype(BF16)
    return sig, qaux, kaux


def _moba_kernel(n_cast, sig_ref, q_ref, k_ref, vt_ref, qaux_ref, kaux_ref, *refs):
    cast_in, (o_ref, *cast_out), scratch = refs[:n_cast], refs[n_cast:2 * n_cast + 1], refs[2 * n_cast + 1:]
    kmean_ref, kaug_ref, kms_ref, sel_ref, acc_ref, m_ref, s_ref, mx_ref = scratch
    for src, dst in zip(cast_in, cast_out):
        dst[...] = src[...].astype(BF16)
    hg = pl.program_id(1)
    i = pl.program_id(2)
    bs = MOBA_BLOCK
    nb = k_ref.shape[1] // bs
    hd = MOBA_HEAD_DIM
    va = MOBA_VALUE_ROWS
    n_heads = q_ref.shape[-1] // hd
    n_pairs = n_heads // 2
    pair_lanes = [slice(p * LANES, (p + 1) * LANES) for p in range(n_pairs)]
    gate_rows = 3 * nb

    @pl.when(i == 0)
    def _():
        lane = lax.broadcasted_iota(jnp.int32, (1, LANES), 1)
        first = lane < hd
        for n in range(nb):
            kb = k_ref[0, n * bs:(n + 1) * bs, :]
            kmean_ref[n:n + 1, :] = jnp.sum(kb.astype(F32), axis=0, keepdims=True) * (1.0 / bs)
            for p in range(n_pairs):
                kp = kb[:, pair_lanes[p]]
                zero = jnp.zeros_like(kp)
                for h, kh in enumerate((jnp.where(first, kp, zero), jnp.where(first, zero, kp))):
                    rows = slice((2 * n + h) * bs, (2 * n + h + 1) * bs)
                    kaug_ref[p, rows, :] = jnp.concatenate([kh, kaux_ref[h]], axis=1)
        pieces = _split3(kmean_ref[...])
        for p in range(n_pairs):
            for h in range(2):
                for r, piece in enumerate(pieces):
                    pp = piece[:, pair_lanes[p]]
                    zero = jnp.zeros_like(pp)
                    ph = jnp.where(first, pp, zero) if h == 0 else jnp.where(first, zero, pp)
                    rows = slice(h * gate_rows + r * nb, h * gate_rows + (r + 1) * nb)
                    kms_ref[p, rows, :] = jnp.concatenate([ph, zero], axis=1)

    qaug = [jnp.concatenate([q_ref[0, :, pair_lanes[p]], qaux_ref[p]], axis=1) for p in range(n_pairs)]
    sig = [sig_ref[hg * n_heads + h] for h in range(n_heads)]

    def select_blocks(h, g):
        nidx = lax.broadcasted_iota(jnp.int32, (nb, bs), 0)
        g = jnp.where(nidx < i, g, NEG_INF)
        sel = jnp.zeros((nb, bs), jnp.bool_)
        for _ in range(MOBA_TOPK):
            best = jnp.max(g, axis=0, keepdims=True)
            first = jnp.min(jnp.where(g == best, nidx, nb), axis=0, keepdims=True)
            pick = nidx == first
            sel = sel | pick
            g = jnp.where(pick, NEG_INF, g)
        sel_ref[h] = jnp.where(sel & (nidx < i), 0.0, NEG_INF)

    def stage_a(blk, slot, p, with_gate=False):
        start = pl.multiple_of(blk * (2 * bs), 2 * bs)
        lhs = kaug_ref[p, pl.ds(start, 2 * bs), :]
        if with_gate:
            lhs = jnp.concatenate([lhs, kms_ref[p]], axis=0)
        t = _dot_nt(lhs, qaug[p])
        for h in range(2):
            if with_gate:
                g0 = 2 * bs + h * gate_rows
                select_blocks(2 * p + h, t[g0:g0 + nb] + t[g0 + nb:g0 + 2 * nb] + t[g0 + 2 * nb:g0 + 3 * nb])
            th = t[h * bs:(h + 1) * bs]
            s_ref[slot, 2 * p + h] = th
            mx_ref[slot, 2 * p + h:2 * p + h + 1, :] = jnp.max(th, axis=0, keepdims=True)

    def values(blk, h):
        start = pl.multiple_of(blk * bs, bs)
        return vt_ref[0, h * va:(h + 1) * va, pl.ds(start, bs)]

    def stage_b_own(slot, p):
        krow = lax.broadcasted_iota(jnp.int32, (bs, bs), 0)
        qcol = lax.broadcasted_iota(jnp.int32, (bs, bs), 1)
        causal = qcol >= krow
        for h in (2 * p, 2 * p + 1):
            t = jnp.where(causal, s_ref[slot, h], NEG_INF)
            m = jnp.max(t, axis=0, keepdims=True)
            pr = jnp.exp2(t - m)
            m_ref[h:h + 1, :] = m
            acc_ref[h] = _dot(values(i, h), pr.astype(BF16))

    def stage_b_past(blk, slot, p):
        off = ((i - blk) * bs).astype(F32)
        for h in (2 * p, 2 * p + 1):
            cs = sel_ref[h, pl.ds(blk, 1), :] - sig[h] * off
            m_old = m_ref[h:h + 1, :]
            m_new = jnp.maximum(m_old, mx_ref[slot, h:h + 1, :] + cs)
            alpha = jnp.exp2(m_old - m_new)
            pr = jnp.exp2(s_ref[slot, h] + (cs - m_new))
            m_ref[h:h + 1, :] = m_new
            acc_ref[h] = alpha * acc_ref[h] + _dot(values(blk, h), pr.astype(BF16))

    for p in range(n_pairs):
        stage_a(i, 0, p, with_gate=True)
    for p in range(n_pairs):
        stage_a(jnp.maximum(i - 1, 0), 1, p)
        stage_b_own(0, p)

    def step(u, carry):
        blk = i - 1 - 2 * u
        for p in range(n_pairs):
            stage_a(blk - 1, 0, p)
            stage_b_past(blk, 1, p)
        for p in range(n_pairs):
            stage_a(jnp.maximum(blk - 2, 0), 1, p)
            stage_b_past(blk - 1, 0, p)
        return carry

    lax.fori_loop(0, i // 2, step, 0)

    @pl.when(i % 2 == 1)
    def _():
        for p in range(n_pairs):
            stage_b_past(0, 1, p)

    for p in range(n_pairs):
        ot = jnp.concatenate([acc_ref[2 * p + h, :hd, :] / acc_ref[2 * p + h, hd:hd + 1, :] for h in range(2)], axis=0)
        o_ref[0, :, pair_lanes[p]] = ot.T.astype(BF16)


def _moba(qa, ka, vat, cast_weights=(), heads_per_step=8):
    B, S, W = qa.shape
    bs = MOBA_BLOCK
    nb = S // bs
    n_heads = W // MOBA_HEAD_DIM
    gw = heads_per_step * MOBA_HEAD_DIM
    sig, qaux, kaux = _alibi_tables(n_heads, bs)
    n_steps = B * (W // gw) * nb
    cast_specs = [pl.BlockSpec((w.shape[0] // n_steps, w.shape[1]), lambda b, hg, i: ((b * (W // gw) + hg) * nb + i, 0))
                  for w in cast_weights]
    outs = pl.pallas_call(
        functools.partial(_moba_kernel, len(cast_weights)),
        grid=(B, W // gw, nb),
        in_specs=[
            pl.BlockSpec(memory_space=pltpu.SMEM),
            pl.BlockSpec((1, bs, gw), lambda b, hg, i: (b, i, hg)),
            pl.BlockSpec((1, S, gw), lambda b, hg, i: (b, 0, hg)),
            pl.BlockSpec((1, heads_per_step * MOBA_VALUE_ROWS, S), lambda b, hg, i: (b, hg, 0)),
            pl.BlockSpec((heads_per_step // 2, bs, LANES), lambda b, hg, i: (hg, 0, 0)),
            pl.BlockSpec((2, bs, LANES), lambda b, hg, i: (0, 0, 0)),
        ] + cast_specs,
        out_specs=[pl.BlockSpec((1, bs, gw), lambda b, hg, i: (b, i, hg))] + cast_specs,
        out_shape=[jax.ShapeDtypeStruct((B, S, W), BF16)] + [jax.ShapeDtypeStruct(w.shape, BF16) for w in cast_weights],
        scratch_shapes=[
            pltpu.VMEM((nb, gw), F32),
            pltpu.VMEM((heads_per_step // 2, 2 * S, 2 * LANES), BF16),
            pltpu.VMEM((heads_per_step // 2, 6 * nb, 2 * LANES), BF16),
            pltpu.VMEM((heads_per_step, nb, bs), F32),
            pltpu.VMEM((heads_per_step, MOBA_VALUE_ROWS, bs), F32),
            pltpu.VMEM((8, bs), F32),
            pltpu.VMEM((2, heads_per_step, bs, bs), F32),
            pltpu.VMEM((2, 8, bs), F32),
        ],
        compiler_params=pltpu.CompilerParams(
            dimension_semantics=("arbitrary", "arbitrary", "arbitrary"), vmem_limit_bytes=VMEM_LIMIT_BYTES),
        name="moba_attention",
    )(sig, qa, ka, vat, qaux, kaux, *cast_weights)
    return outs[0], outs[1:]


def _gla_kernel(q_ref, k_ref, la_ref, v_ref, r_ref, gn_ref, o_ref, st_ref):
    s_idx = pl.program_id(1)
    C = GLA_CHUNK
    tc = q_ref.shape[1]

    @pl.when(s_idx == 0)
    def _():
        st_ref[...] = jnp.zeros_like(st_ref)

    G = GLA_GROUP
    n_pairs = GLA_HEADS // 2
    lane = lax.broadcasted_iota(jnp.int32, (1, LANES), 1)
    head_lanes = (lane < GLA_KEY_DIM, lane >= GLA_KEY_DIM)
    ti = lax.broadcasted_iota(jnp.int32, (G, G), 0)
    si = lax.broadcasted_iota(jnp.int32, (G, G), 1)
    causal = (ti // C == si // C) & (si <= ti)
    tril = jnp.where(causal, 1.0, 0.0).astype(BF16)
    gn = gn_ref[...]

    cpg = G // C
    n_groups = tc // G
    grp_rows = [slice(g * G, (g + 1) * G) for g in range(n_groups)]
    chunk_rows = [slice(c * C, (c + 1) * C) for c in range(cpg)]
    pair_lanes = [slice(hp * LANES, (hp + 1) * LANES) for hp in range(n_pairs)]
    val_lanes = [slice(hd * GLA_VAL_DIM, (hd + 1) * GLA_VAL_DIM) for hd in range(GLA_HEADS)]

    b = [sum(_dot(tril, piece) for piece in _split3(la_ref[0, rows, :])) for rows in grp_rows]

    q_m, k_inv, k_dec, decay = [], [], [], []
    for g, rows in enumerate(grp_rows):
        b_last = [b[g][c * C + C - 1:c * C + C, :] for c in range(cpg)]
        b_end = jnp.concatenate([jnp.broadcast_to(bl, (C, bl.shape[1])) for bl in b_last], axis=0)
        q_dec = q_ref[0, rows, :] * jnp.exp(b[g])
        ks = k_ref[0, rows, :]
        k_inv.append((ks * jnp.exp(-b[g])).astype(BF16))
        k_dec.append((ks * jnp.exp(b_end - b[g])).astype(BF16))
        decay.append([jnp.exp(bl) for bl in b_last])
        q_m.append([jnp.where(head_lanes[hd % 2], q_dec[:, pair_lanes[hd // 2]], 0.0).astype(BF16)
                    for hd in range(GLA_HEADS)])

    attn = [[_dot_nt(q_m[g][hd], k_inv[g][:, pair_lanes[hd // 2]]) for hd in range(GLA_HEADS)]
            for g in range(n_groups)]

    update = [[[None] * cpg for _ in range(n_pairs)] for _ in range(n_groups)]
    for g, rows in enumerate(grp_rows):
        for hp in range(n_pairs):
            v_pair = v_ref[0, rows, hp * 2 * GLA_VAL_DIM:(hp + 1) * 2 * GLA_VAL_DIM]
            for c, cr in enumerate(chunk_rows):
                u = _dot_tn(v_pair[cr], k_dec[g][cr, pair_lanes[hp]])
                update[g][hp][c] = jnp.where(head_lanes[0], u[:GLA_VAL_DIM], u[GLA_VAL_DIM:])

    o_intra = [[_dot(jnp.where(causal, attn[g][hd], 0.0).astype(BF16), v_ref[0, grp_rows[g], val_lanes[hd]])
                for hd in range(GLA_HEADS)] for g in range(n_groups)]

    states = [st_ref[hp] for hp in range(n_pairs)]
    for g, rows in enumerate(grp_rows):
        o_inter = [[None] * cpg for _ in range(GLA_HEADS)]
        for c, cr in enumerate(chunk_rows):
            for hp in range(n_pairs):
                q_pair = jnp.concatenate([q_m[g][2 * hp][cr], q_m[g][2 * hp + 1][cr]], axis=0)
                oi = _dot_nt(q_pair, states[hp].astype(BF16))
                o_inter[2 * hp][c] = oi[:C]
                o_inter[2 * hp + 1][c] = oi[C:]
                states[hp] = states[hp] * decay[g][c][:, pair_lanes[hp]] + update[g][hp][c]
        for hd in range(GLA_HEADS):
            o = jnp.concatenate(o_inter[hd], axis=0) + o_intra[g][hd]
            ms = jnp.mean(o * o, axis=-1, keepdims=True)
            o_ref[0, rows, val_lanes[hd]] = (o * lax.rsqrt(ms + RMS_EPS) * gn * r_ref[0, rows, val_lanes[hd]]).astype(BF16)

    for hp in range(n_pairs):
        st_ref[hp] = states[hp]


def _gla(qg, kg, la, vg, rg, gnorm, tc=512):
    B, S, kw = qg.shape
    vw = vg.shape[-1]
    row = lambda w: pl.BlockSpec((1, tc, w), lambda b, s: (b, s, 0))
    return pl.pallas_call(
        _gla_kernel,
        grid=(B, S // tc),
        in_specs=[row(kw), row(kw), row(kw), row(vw), row(vw),
                  pl.BlockSpec((1, GLA_VAL_DIM), lambda b, s: (0, 0))],
        out_specs=row(vw),
        out_shape=jax.ShapeDtypeStruct((B, S, vw), BF16),
        scratch_shapes=[pltpu.VMEM((GLA_HEADS // 2, GLA_VAL_DIM, LANES), F32)],
        compiler_params=pltpu.CompilerParams(
            dimension_semantics=("arbitrary", "arbitrary"), vmem_limit_bytes=VMEM_LIMIT_BYTES),
        name="gla_scan",
    )(qg, kg, la, vg, rg, gnorm)


def _memkv_kernel(mem_ref, w_ref, k_ref, v_ref):
    kv = _dot(mem_ref[0].astype(BF16), w_ref[...])
    d = k_ref.shape[-1]
    k_ref[0] = kv[:, :d].astype(BF16)
    v_ref[0] = kv[:, d:].astype(BF16)


def _mem_kv(mem, w_xkv):
    B, M, D = mem.shape
    wb = w_xkv.astype(BF16)
    blk = pl.BlockSpec((1, M, D), lambda b: (b, 0, 0))
    return pl.pallas_call(
        _memkv_kernel,
        grid=(B,),
        in_specs=[blk, pl.BlockSpec(wb.shape, lambda b: (0, 0))],
        out_specs=(blk, blk),
        out_shape=(jax.ShapeDtypeStruct((B, M, D), BF16), jax.ShapeDtypeStruct((B, M, D), BF16)),
        compiler_params=pltpu.CompilerParams(
            dimension_semantics=("arbitrary",), vmem_limit_bytes=VMEM_LIMIT_BYTES),
        name="mem_kv_projection",
    )(mem, wb)


def _post_mixer_kernel(alpha, x_ref, mo_ref, go_ref, wm_ref, g1_ref, b1_ref, wxq_ref,
                       km_ref, vm_ref, wxo_ref, g2_ref, b2_ref, o_ref):
    tm, d = x_ref.shape[1], x_ref.shape[2]
    hd = d // XATTN_HEADS
    subs = [slice(r * POST_SUB_ROWS, (r + 1) * POST_SUB_ROWS) for r in range(tm // POST_SUB_ROWS)]
    heads = [slice(h * hd, (h + 1) * hd) for h in range(XATTN_HEADS)]
    aw = mo_ref.shape[-1]
    mix = [_dot(mo_ref[0, r, :], wm_ref[:aw, :]) + _dot(go_ref[0, r, :], wm_ref[aw:, :]) for r in subs]
    x1 = [_layer_norm(alpha * x_ref[0, r, :] + mix[i], g1_ref[...], b1_ref[...]) for i, r in enumerate(subs)]
    q = [(_dot(x1[i].astype(BF16), wxq_ref[...]) * (hd ** -0.5)).astype(BF16) for i in range(len(subs))]
    s = [[_dot_nt(q[i][:, c], km_ref[0, :, c]) for c in heads] for i in range(len(subs))]
    oh = []
    for i in range(len(subs)):
        row = []
        for h, c in enumerate(heads):
            m = jnp.max(s[i][h], axis=-1, keepdims=True)
            p = jnp.exp(s[i][h] - m)
            l = jnp.sum(p, axis=-1, keepdims=True)
            row.append((_dot(p.astype(BF16), vm_ref[0, :, c]) / l).astype(BF16))
        oh.append(row)
    for i, r in enumerate(subs):
        xa = sum(_dot(oh[i][h], wxo_ref[c, :]) for h, c in enumerate(heads))
        o_ref[0, r, :] = _layer_norm(alpha * x1[i] + xa, g2_ref[...], b2_ref[...])


def _post_mixer(x, moba_o, gla_o, w_mix_o, ln1_g, ln1_b, w_xq, kmem, vmem, w_xo, ln2_g, ln2_b, alpha, tm=1024):
    B, S, D = x.shape
    aw = moba_o.shape[-1]
    M = kmem.shape[1]
    wm = w_mix_o.astype(BF16)
    wxq = w_xq.astype(BF16)
    wxo = w_xo.astype(BF16)
    vec = lambda a: a.reshape(1, D)
    row = lambda w: pl.BlockSpec((1, tm, w), lambda b, s: (b, s, 0))
    full = lambda a: pl.BlockSpec(a.shape, lambda b, s: (0,) * a.ndim)
    vspec = pl.BlockSpec((1, D), lambda b, s: (0, 0))
    memspec = pl.BlockSpec((1, M, D), lambda b, s: (b, 0, 0))
    return pl.pallas_call(
        functools.partial(_post_mixer_kernel, alpha),
        grid=(B, S // tm),
        in_specs=[row(D), row(aw), row(gla_o.shape[-1]), full(wm), vspec, vspec, full(wxq),
                  memspec, memspec, full(wxo), vspec, vspec],
        out_specs=row(D),
        out_shape=jax.ShapeDtypeStruct((B, S, D), F32),
        compiler_params=pltpu.CompilerParams(
            dimension_semantics=("arbitrary", "arbitrary"), vmem_limit_bytes=VMEM_LIMIT_BYTES),
        name="post_mixer",
    )(x, moba_o, gla_o, wm, vec(ln1_g), vec(ln1_b), wxq, kmem, vmem, wxo, vec(ln2_g), vec(ln2_b))


def _mlp_kernel(alpha, fc, x_ref, w1_ref, w2_ref, g_ref, b_ref, o_ref):
    for r in range(x_ref.shape[1] // MLP_SUB_ROWS):
        rows = slice(r * MLP_SUB_ROWS, (r + 1) * MLP_SUB_ROWS)
        x = x_ref[0, rows, :]
        xb = x.astype(BF16)
        acc = jnp.zeros_like(x)
        for c in range(w1_ref.shape[1] // fc):
            cols = slice(c * fc, (c + 1) * fc)
            h = jnp.maximum(_dot(xb, w1_ref[:, cols]), 0.0)
            acc = acc + _dot((h * h).astype(BF16), w2_ref[cols, :])
        o_ref[0, rows, :] = _layer_norm(alpha * x + acc, g_ref[...], b_ref[...])


def _mlp(x, w_ff1, w_ff2, g, b, alpha, tm=1024, fc=512):
    B, S, D = x.shape
    w1 = w_ff1.astype(BF16)
    w2 = w_ff2.astype(BF16)
    row = pl.BlockSpec((1, tm, D), lambda bb, s: (bb, s, 0))
    full = lambda a: pl.BlockSpec(a.shape, lambda bb, s: (0, 0), pipeline_mode=pl.Buffered(1))
    vspec = pl.BlockSpec((1, D), lambda bb, s: (0, 0))
    return pl.pallas_call(
        functools.partial(_mlp_kernel, alpha, fc),
        grid=(B, S // tm),
        in_specs=[row, full(w1), full(w2), vspec, vspec],
        out_specs=row,
        out_shape=jax.ShapeDtypeStruct((B, S, D), F32),
        compiler_params=pltpu.CompilerParams(
            dimension_semantics=("arbitrary", "arbitrary"), vmem_limit_bytes=VMEM_LIMIT_BYTES),
        name="sq_relu_mlp",
    )(x, w1, w2, g.reshape(1, D), b.reshape(1, D))


def kernel(x, mem, w_in, w_gate_up, b_gate, gla_norm_g, w_mix_o, ln1_g, ln1_b, w_xq, w_xkv, w_xo,
           ln2_g, ln2_b, w_ff1, w_ff2, ln3_g, ln3_b):
    depth = w_in.shape[0]
    alpha = (2.0 * depth) ** 0.25
    for l in range(depth):
        qa, ka, vat, qg, kg, vg, rg, la = _in_projection(x, w_in[l], w_gate_up[l], b_gate[l])
        moba_o, (wkv, wmix, wxq, wxo, wf1, wf2) = _moba(
            qa, ka, vat, cast_weights=(w_xkv[l], w_mix_o[l], w_xq[l], w_xo[l], w_ff1[l], w_ff2[l]))
        gla_o = _gla(qg, kg, la, vg, rg, gla_norm_g[l].reshape(1, GLA_VAL_DIM))
        kmem, vmem = _mem_kv(mem, wkv)
        x = _post_mixer(x, moba_o, gla_o, wmix, ln1_g[l], ln1_b[l], wxq, kmem, vmem, wxo, ln2_g[l], ln2_b[l], alpha)
        x = _mlp(x, wf1, wf2, ln3_g[l], ln3_b[l], alpha)
    return x
```

```python
import functools

import jax
import jax.numpy as jnp
import numpy as np
from jax import lax
from jax.experimental import pallas as pl
from jax.experimental.pallas import tpu as pltpu

F32 = jnp.float32
BF16 = jnp.bfloat16

MOBA_HEAD_DIM = 64
MOBA_VALUE_ROWS = 80
MOBA_BLOCK = 256
MOBA_TOPK = 3
GLA_HEADS = 4
GLA_KEY_DIM = 64
GLA_VAL_DIM = 128
GLA_GATE_RANK = 16
GLA_GATE_TEMP = 16.0
GATE_TERMS = 6
GLA_CHUNK = 64
GLA_GROUP = 256
XATTN_HEADS = 4
POST_SUB_ROWS = 256
WEIGHT_CAST_TILE = 256
MLP_SUB_ROWS = 512
LN_EPS = 1e-5
RMS_EPS = 1e-6

LANES = 128
VMEM_LIMIT_BYTES = 56 * 1024 * 1024

NEG_INF = float("-inf")
LOG2E = 1.4426950408889634


def _dot(a, b):
    return jnp.dot(a, b, preferred_element_type=F32)


def _dot_nt(a, b):
    return lax.dot_general(a, b, (((1,), (1,)), ((), ())), preferred_element_type=F32)


def _dot_tn(a, b):
    return lax.dot_general(a, b, (((0,), (0,)), ((), ())), preferred_element_type=F32)


def _split3(a):
    hi = a.astype(BF16)
    r1 = a - hi.astype(F32)
    mid = r1.astype(BF16)
    lo = (r1 - mid.astype(F32)).astype(BF16)
    return hi, mid, lo


def _layer_norm(x, g, b):
    mu = jnp.mean(x, axis=-1, keepdims=True)
    xc = x - mu
    var = jnp.mean(xc * xc, axis=-1, keepdims=True)
    return xc * lax.rsqrt(var + LN_EPS) * g + b


def _log_sigmoid(z):
    return jnp.minimum(z, 0.0) - jnp.log(1.0 + jnp.exp(-jnp.abs(z)))


def _sigmoid(z):
    return 1.0 / (1.0 + jnp.exp(-z))


def _inproj_kernel(x_ref, wt_ref, wz_ref, bg_ref,
                   qa_ref, ka_ref, vat_ref, qg_ref, kg_ref, vg_ref, rg_ref, la_ref,
                   wqk_ref, wvat_ref, wg_ref, wvr_ref):
    aw, kw, vw = qa_ref.shape[-1], qg_ref.shape[-1], vg_ref.shape[-1]

    @pl.when((pl.program_id(0) == 0) & (pl.program_id(1) == 0))
    def _():
        def convert(dst, first_row):
            for r in range(0, dst.shape[0], WEIGHT_CAST_TILE):
                n = min(WEIGHT_CAST_TILE, dst.shape[0] - r)
                dst[r:r + n, :] = wt_ref[first_row + r:first_row + r + n, :].astype(BF16)

        convert(wqk_ref, 0)
        convert(wvat_ref, 2 * aw)
        convert(wg_ref.at[:2 * kw], 3 * aw)
        convert(wvr_ref, 3 * aw + 2 * kw)
        glr_row = 3 * aw + 2 * kw + 2 * vw
        glr = wt_ref[glr_row:glr_row + GLA_GATE_RANK, :].astype(BF16)
        for j in range(LANES // GLA_GATE_RANK):
            rows = slice(2 * kw + j * GLA_GATE_RANK, 2 * kw + (j + 1) * GLA_GATE_RANK)
            wg_ref[rows, :] = glr if j < GATE_TERMS else jnp.zeros_like(glr)

    xb = x_ref[0].astype(BF16)
    qk = _dot_nt(xb, wqk_ref[...])
    qa_ref[0] = (qk[:, :aw] * (MOBA_HEAD_DIM ** -0.5 * LOG2E)).astype(BF16)
    ka_ref[0] = qk[:, aw:].astype(BF16)
    vt = _dot_nt(wvat_ref[...], xb).astype(BF16)
    hd, va = MOBA_HEAD_DIM, MOBA_VALUE_ROWS
    for h in range(aw // hd):
        vat_ref[0, h * va:h * va + hd, :] = vt[h * hd:(h + 1) * hd, :]
        vat_ref[0, h * va + hd:(h + 1) * va, :] = jnp.ones((va - hd, vt.shape[1]), BF16)
    g = _dot_nt(xb, wg_ref[...])
    qg_ref[0] = g[:, :kw] * (GLA_KEY_DIM ** -0.5)
    kg_ref[0] = g[:, kw:2 * kw]
    vr = _dot_nt(xb, wvr_ref[...])
    vg_ref[0] = vr[:, :vw].astype(BF16)
    rg = vr[:, vw:]
    rg_ref[0] = rg * _sigmoid(rg)
    gh, gm, gl = _split3(g[:, 2 * kw:])
    grp = lax.broadcasted_iota(jnp.int32, (1, LANES), 1) // GLA_GATE_RANK
    lhs = jnp.where((grp == 1) | (grp == 4), gm, jnp.where(grp == 2, gl, gh))
    z = _dot(lhs, wz_ref[...]) + bg_ref[...]
    la_ref[0] = _log_sigmoid(z) / GLA_GATE_TEMP


def _in_projection(x, w_in, w_gate_up, b_gate, tm=1024):
    B, S, D = x.shape
    aw = 512
    vrows = aw // MOBA_HEAD_DIM * MOBA_VALUE_ROWS
    kw = GLA_HEADS * GLA_KEY_DIM
    vw = GLA_HEADS * GLA_VAL_DIM
    wt = jnp.swapaxes(w_in, 0, 1)
    wh, wm, wl = _split3(w_gate_up)
    pad = LANES - GATE_TERMS * GLA_GATE_RANK
    wz = jnp.pad(jnp.concatenate([wh, wh, wh, wm, wm, wl], axis=0), ((0, pad), (0, 0)))
    bg = b_gate.reshape(1, kw)

    def full(a):
        return pl.BlockSpec(a.shape, lambda b, s: (0,) * a.ndim)

    row = lambda w: pl.BlockSpec((1, tm, w), lambda b, s: (b, s, 0))
    out_shape = (
        jax.ShapeDtypeStruct((B, S, aw), BF16),
        jax.ShapeDtypeStruct((B, S, aw), BF16),
        jax.ShapeDtypeStruct((B, vrows, S), BF16),
        jax.ShapeDtypeStruct((B, S, kw), F32),
        jax.ShapeDtypeStruct((B, S, kw), F32),
        jax.ShapeDtypeStruct((B, S, vw), BF16),
        jax.ShapeDtypeStruct((B, S, vw), F32),
        jax.ShapeDtypeStruct((B, S, kw), F32),
    )
    out_specs = (row(aw), row(aw), pl.BlockSpec((1, vrows, tm), lambda b, s: (b, 0, s)),
                 row(kw), row(kw), row(vw), row(vw), row(kw))
    return pl.pallas_call(
        _inproj_kernel,
        grid=(B, S // tm),
        in_specs=[row(D), pl.BlockSpec(wt.shape, lambda b, s: (0, 0), pipeline_mode=pl.Buffered(1)),
                  full(wz), full(bg)],
        out_specs=out_specs,
        out_shape=out_shape,
        scratch_shapes=[
            pltpu.VMEM((2 * aw, D), BF16),
            pltpu.VMEM((aw, D), BF16),
            pltpu.VMEM((2 * kw + LANES, D), BF16),
            pltpu.VMEM((2 * vw, D), BF16),
        ],
        compiler_params=pltpu.CompilerParams(
            dimension_semantics=("arbitrary", "arbitrary"), vmem_limit_bytes=VMEM_LIMIT_BYTES),
        name="in_projection",
    )(x, wt, wz, bg)


def _alibi_tables(n_heads, bs):
    def split3(v):
        pieces = []
        for _ in range(3):
            piece = v.astype(BF16)
            pieces.append(piece)
            v = v - piece.astype(np.float64)
        return pieces

    slopes = 2.0 ** (-8.0 * np.arange(1, n_heads + 1, dtype=np.float64) / n_heads)
    sig = slopes * LOG2E
    rel = np.arange(bs, dtype=np.float64)
    a = -sig[:, None] * rel[None, :]
    cols = np.stack(split3(a) + split3(np.broadcast_to(sig[:, None], a.shape)), axis=-1)
    qaux = np.concatenate([cols[0::2], cols[1::2]], axis=-1)
    qaux = np.pad(qaux, ((0, 0), (0, 0), (0, LANES - qaux.shape[-1])))
    one = np.ones((bs,), np.float64)
    kcols = np.stack([one, one, one, rel, rel, rel], axis=-1)
    zeros = np.zeros_like(kcols)
    kaux = np.stack([np.concatenate([kcols, zeros], axis=-1), np.concatenate([zeros, kcols], axis=-1)])
    kaux = np.pad(kaux, ((0, 0), (0, 0), (0, LANES - kaux.shape[-1]))).astype(BF16)
    return jnp.asarray(sig, F32), jnp.asarray(qaux), jnp.asarray(kaux)


def _moba_kernel(n_cast, sig_ref, q_ref, k_ref, vt_ref, qaux_ref, kaux_ref, *refs):
    cast_in, (o_ref, *cast_out), scratch = refs[:n_cast], refs[n_cast:2 * n_cast + 1], refs[2 * n_cast + 1:]
    kmean_ref, kaug_ref, kms_ref, sel_ref, acc_ref, m_ref, s_ref, mx_ref = scratch
    for src, dst in zip(cast_in, cast_out):
        dst[...] = src[...].astype(BF16)
    hg = pl.program_id(1)
    i = pl.program_id(2)
    bs = MOBA_BLOCK
    nb = k_ref.shape[1] // bs
    hd = MOBA_HEAD_DIM
    va = MOBA_VALUE_ROWS
    n_heads = q_ref.shape[-1] // hd
    n_pairs = n_heads // 2
    pair_lanes = [slice(p * LANES, (p + 1) * LANES) for p in range(n_pairs)]
    gate_rows = 3 * nb

    @pl.when(i == 0)
    def _():
        lane = lax.broadcasted_iota(jnp.int32, (1, LANES), 1)
        first = lane < hd
        for n in range(nb):
            kb = k_ref[0, n * bs:(n + 1) * bs, :]
            kmean_ref[n:n + 1, :] = jnp.sum(kb.astype(F32), axis=0, keepdims=True) * (1.0 / bs)
            for p in range(n_pairs):
                kp = kb[:, pair_lanes[p]]
                zero = jnp.zeros_like(kp)
                for h, kh in enumerate((jnp.where(first, kp, zero), jnp.where(first, zero, kp))):
                    rows = slice((2 * n + h) * bs, (2 * n + h + 1) * bs)
                    kaug_ref[p, rows, :] = jnp.concatenate([kh, kaux_ref[h]], axis=1)
        pieces = _split3(kmean_ref[...])
        for p in range(n_pairs):
            for h in range(2):
                for r, piece in enumerate(pieces):
                    pp = piece[:, pair_lanes[p]]
                    zero = jnp.zeros_like(pp)
                    ph = jnp.where(first, pp, zero) if h == 0 else jnp.where(first, zero, pp)
                    rows = slice(h * gate_rows + r * nb, h * gate_rows + (r + 1) * nb)
                    kms_ref[p, rows, :] = jnp.concatenate([ph, zero], axis=1)

    qaug = [jnp.concatenate([q_ref[0, :, pair_lanes[p]], qaux_ref[p]], axis=1) for p in range(n_pairs)]
    sig = [sig_ref[hg * n_heads + h] for h in range(n_heads)]

    def select_blocks(h, g):
        nidx = lax.broadcasted_iota(jnp.int32, (nb, bs), 0)
        g = jnp.where(nidx < i, g, NEG_INF)
        sel = jnp.zeros((nb, bs), jnp.bool_)
        for _ in range(MOBA_TOPK):
            best = jnp.max(g, axis=0, keepdims=True)
            first = jnp.min(jnp.where(g == best, nidx, nb), axis=0, keepdims=True)
            pick = nidx == first
            sel = sel | pick
            g = jnp.where(pick, NEG_INF, g)
        sel_ref[h] = jnp.where(sel & (nidx < i), 0.0, NEG_INF)

    def stage_a(blk, slot, p, with_gate=False):
        start = pl.multiple_of(blk * (2 * bs), 2 * bs)
        lhs = kaug_ref[p, pl.ds(start, 2 * bs), :]
        if with_gate:
            lhs = jnp.concatenate([lhs, kms_ref[p]], axis=0)
        t = _dot_nt(lhs, qaug[p])
        for h in range(2):
            if with_gate:
                g0 = 2 * bs + h * gate_rows
                select_blocks(2 * p + h, t[g0:g0 + nb] + t[g0 + nb:g0 + 2 * nb] + t[g0 + 2 * nb:g0 + 3 * nb])
            th = t[h * bs:(h + 1) * bs]
            s_ref[slot, 2 * p + h] = th
            mx_ref[slot, 2 * p + h:2 * p + h + 1, :] = jnp.max(th, axis=0, keepdims=True)

    def values(blk, h):
        start = pl.multiple_of(blk * bs, bs)
        return vt_ref[0, h * va:(h + 1) * va, pl.ds(start, bs)]

    def stage_b_own(slot, p):
        krow = lax.broadcasted_iota(jnp.int32, (bs, bs), 0)
        qcol = lax.broadcasted_iota(jnp.int32, (bs, bs), 1)
        causal = qcol >= krow
        for h in (2 * p, 2 * p + 1):
            t = jnp.where(causal, s_ref[slot, h], NEG_INF)
            m = jnp.max(t, axis=0, keepdims=True)
            pr = jnp.exp2(t - m)
            m_ref[h:h + 1, :] = m
            acc_ref[h] = _dot(values(i, h), pr.astype(BF16))

    def stage_b_past(blk, slot, p):
        off = ((i - blk) * bs).astype(F32)
        for h in (2 * p, 2 * p + 1):
            cs = sel_ref[h, pl.ds(blk, 1), :] - sig[h] * off
            m_old = m_ref[h:h + 1, :]
            m_new = jnp.maximum(m_old, mx_ref[slot, h:h + 1, :] + cs)
            alpha = jnp.exp2(m_old - m_new)
            pr = jnp.exp2(s_ref[slot, h] + (cs - m_new))
            m_ref[h:h + 1, :] = m_new
            acc_ref[h] = alpha * acc_ref[h] + _dot(values(blk, h), pr.astype(BF16))

    for p in range(n_pairs):
        stage_a(i, 0, p, with_gate=True)
    for p in range(n_pairs):
        stage_a(jnp.maximum(i - 1, 0), 1, p)
        stage_b_own(0, p)

    def step(u, carry):
        blk = i - 1 - 2 * u
        for p in range(n_pairs):
            stage_a(blk - 1, 0, p)
            stage_b_past(blk, 1, p)
        for p in range(n_pairs):
            stage_a(jnp.maximum(blk - 2, 0), 1, p)
            stage_b_past(blk - 1, 0, p)
        return carry

    lax.fori_loop(0, i // 2, step, 0)

    @pl.when(i % 2 == 1)
    def _():
        for p in range(n_pairs):
            stage_b_past(0, 1, p)

    for p in range(n_pairs):
        ot = jnp.concatenate([acc_ref[2 * p + h, :hd, :] / acc_ref[2 * p + h, hd:hd + 1, :] for h in range(2)], axis=0)
        o_ref[0, :, pair_lanes[p]] = ot.T.astype(BF16)


def _moba(qa, ka, vat, cast_weights=(), heads_per_step=8):
    B, S, W = qa.shape
    bs = MOBA_BLOCK
    nb = S // bs
    n_heads = W // MOBA_HEAD_DIM
    gw = heads_per_step * MOBA_HEAD_DIM
    sig, qaux, kaux = _alibi_tables(n_heads, bs)
    n_steps = B * (W // gw) * nb
    cast_specs = [pl.BlockSpec((w.shape[0] // n_steps, w.shape[1]), lambda b, hg, i: ((b * (W // gw) + hg) * nb + i, 0))
                  for w in cast_weights]
    outs = pl.pallas_call(
        functools.partial(_moba_kernel, len(cast_weights)),
        grid=(B, W // gw, nb),
        in_specs=[
            pl.BlockSpec(memory_space=pltpu.SMEM),
            pl.BlockSpec((1, bs, gw), lambda b, hg, i: (b, i, hg)),
            pl.BlockSpec((1, S, gw), lambda b, hg, i: (b, 0, hg)),
            pl.BlockSpec((1, heads_per_step * MOBA_VALUE_ROWS, S), lambda b, hg, i: (b, hg, 0)),
            pl.BlockSpec((heads_per_step // 2, bs, LANES), lambda b, hg, i: (hg, 0, 0)),
            pl.BlockSpec((2, bs, LANES), lambda b, hg, i: (0, 0, 0)),
        ] + cast_specs,
        out_specs=[pl.BlockSpec((1, bs, gw), lambda b, hg, i: (b, i, hg))] + cast_specs,
        out_shape=[jax.ShapeDtypeStruct((B, S, W), BF16)] + [jax.ShapeDtypeStruct(w.shape, BF16) for w in cast_weights],
        scratch_shapes=[
            pltpu.VMEM((nb, gw), F32),
            pltpu.VMEM((heads_per_step // 2, 2 * S, 2 * LANES), BF16),
            pltpu.VMEM((heads_per_step // 2, 6 * nb, 2 * LANES), BF16),
            pltpu.VMEM((heads_per_step, nb, bs), F32),
            pltpu.VMEM((heads_per_step, MOBA_VALUE_ROWS, bs), F32),
            pltpu.VMEM((8, bs), F32),
            pltpu.VMEM((2, heads_per_step, bs, bs), F32),
            pltpu.VMEM((2, 8, bs), F32),
        ],
        compiler_params=pltpu.CompilerParams(
            dimension_semantics=("arbitrary", "arbitrary", "arbitrary"), vmem_limit_bytes=VMEM_LIMIT_BYTES),
        name="moba_attention",
    )(sig, qa, ka, vat, qaux, kaux, *cast_weights)
    return outs[0], outs[1:]


def _gla_kernel(q_ref, k_ref, la_ref, v_ref, r_ref, gn_ref, o_ref, st_ref):
    s_idx = pl.program_id(1)
    C = GLA_CHUNK
    tc = q_ref.shape[1]

    @pl.when(s_idx == 0)
    def _():
        st_ref[...] = jnp.zeros_like(st_ref)

    G = GLA_GROUP
    n_pairs = GLA_HEADS // 2
    lane = lax.broadcasted_iota(jnp.int32, (1, LANES), 1)
    head_lanes = (lane < GLA_KEY_DIM, lane >= GLA_KEY_DIM)
    ti = lax.broadcasted_iota(jnp.int32, (G, G), 0)
    si = lax.broadcasted_iota(jnp.int32, (G, G), 1)
    causal = (ti // C == si // C) & (si <= ti)
    tril = jnp.where(causal, 1.0, 0.0).astype(BF16)
    gn = gn_ref[...]

    cpg = G // C
    n_groups = tc // G
    grp_rows = [slice(g * G, (g + 1) * G) for g in range(n_groups)]
    chunk_rows = [slice(c * C, (c + 1) * C) for c in range(cpg)]
    pair_lanes = [slice(hp * LANES, (hp + 1) * LANES) for hp in range(n_pairs)]
    val_lanes = [slice(hd * GLA_VAL_DIM, (hd + 1) * GLA_VAL_DIM) for hd in range(GLA_HEADS)]

    b = [sum(_dot(tril, piece) for piece in _split3(la_ref[0, rows, :])) for rows in grp_rows]

    q_m, k_inv, k_dec, decay = [], [], [], []
    for g, rows in enumerate(grp_rows):
        b_last = [b[g][c * C + C - 1:c * C + C, :] for c in range(cpg)]
        b_end = jnp.concatenate([jnp.broadcast_to(bl, (C, bl.shape[1])) for bl in b_last], axis=0)
        q_dec = q_ref[0, rows, :] * jnp.exp(b[g])
        ks = k_ref[0, rows, :]
        k_inv.append((ks * jnp.exp(-b[g])).astype(BF16))
        k_dec.append((ks * jnp.exp(b_end - b[g])).astype(BF16))
        decay.append([jnp.exp(bl) for bl in b_last])
        q_m.append([jnp.where(head_lanes[hd % 2], q_dec[:, pair_lanes[hd // 2]], 0.0).astype(BF16)
                    for hd in range(GLA_HEADS)])

    attn = [[_dot_nt(q_m[g][hd], k_inv[g][:, pair_lanes[hd // 2]]) for hd in range(GLA_HEADS)]
            for g in range(n_groups)]

    update = [[[None] * cpg for _ in range(n_pairs)] for _ in range(n_groups)]
    for g, rows in enumerate(grp_rows):
        for hp in range(n_pairs):
            v_pair = v_ref[0, rows, hp * 2 * GLA_VAL_DIM:(hp + 1) * 2 * GLA_VAL_DIM]
            for c, cr in enumerate(chunk_rows):
                u = _dot_tn(v_pair[cr], k_dec[g][cr, pair_lanes[hp]])
                update[g][hp][c] = jnp.where(head_lanes[0], u[:GLA_VAL_DIM], u[GLA_VAL_DIM:])

    o_intra = [[_dot(jnp.where(causal, attn[g][hd], 0.0).astype(BF16), v_ref[0, grp_rows[g], val_lanes[hd]])
                for hd in range(GLA_HEADS)] for g in range(n_groups)]

    states = [st_ref[hp] for hp in range(n_pairs)]
    for g, rows in enumerate(grp_rows):
        o_inter = [[None] * cpg for _ in range(GLA_HEADS)]
        for c, cr in enumerate(chunk_rows):
            for hp in range(n_pairs):
                q_pair = jnp.concatenate([q_m[g][2 * hp][cr], q_m[g][2 * hp + 1][cr]], axis=0)
                oi = _dot_nt(q_pair, states[hp].astype(BF16))
                o_inter[2 * hp][c] = oi[:C]
                o_inter[2 * hp + 1][c] = oi[C:]
                states[hp] = states[hp] * decay[g][c][:, pair_lanes[hp]] + update[g][hp][c]
        for hd in range(GLA_HEADS):
            o = jnp.concatenate(o_inter[hd], axis=0) + o_intra[g][hd]
            ms = jnp.mean(o * o, axis=-1, keepdims=True)
            o_ref[0, rows, val_lanes[hd]] = (o * lax.rsqrt(ms + RMS_EPS) * gn * r_ref[0, rows, val_lanes[hd]]).astype(BF16)

    for hp in range(n_pairs):
        st_ref[hp] = states[hp]


def _gla(qg, kg, la, vg, rg, gnorm, tc=1024):
    B, S, kw = qg.shape
    vw = vg.shape[-1]
    row = lambda w: pl.BlockSpec((1, tc, w), lambda b, s: (b, s, 0))
    return pl.pallas_call(
        _gla_kernel,
        grid=(B, S // tc),
        in_specs=[row(kw), row(kw), row(kw), row(vw), row(vw),
                  pl.BlockSpec((1, GLA_VAL_DIM), lambda b, s: (0, 0))],
        out_specs=row(vw),
        out_shape=jax.ShapeDtypeStruct((B, S, vw), BF16),
        scratch_shapes=[pltpu.VMEM((GLA_HEADS // 2, GLA_VAL_DIM, LANES), F32)],
        compiler_params=pltpu.CompilerParams(
            dimension_semantics=("arbitrary", "arbitrary"), vmem_limit_bytes=VMEM_LIMIT_BYTES),
        name="gla_scan",
    )(qg, kg, la, vg, rg, gnorm)


def _post_mixer_kernel(alpha, x_ref, mo_ref, go_ref, wm_ref, g1_ref, b1_ref, wxq_ref,
                       mem_ref, wkv_ref, wxo_ref, g2_ref, b2_ref, o_ref, km_ref, vm_ref):
    @pl.when(pl.program_id(1) == 0)
    def _():
        kv = _dot(mem_ref[0].astype(BF16), wkv_ref[...])
        km_ref[...] = kv[:, :km_ref.shape[-1]].astype(BF16)
        vm_ref[...] = kv[:, km_ref.shape[-1]:].astype(BF16)

    tm, d = x_ref.shape[1], x_ref.shape[2]
    hd = d // XATTN_HEADS
    subs = [slice(r * POST_SUB_ROWS, (r + 1) * POST_SUB_ROWS) for r in range(tm // POST_SUB_ROWS)]
    heads = [slice(h * hd, (h + 1) * hd) for h in range(XATTN_HEADS)]
    aw = mo_ref.shape[-1]
    mix = [_dot(mo_ref[0, r, :], wm_ref[:aw, :]) + _dot(go_ref[0, r, :], wm_ref[aw:, :]) for r in subs]
    x1 = [_layer_norm(alpha * x_ref[0, r, :] + mix[i], g1_ref[...], b1_ref[...]) for i, r in enumerate(subs)]
    q = [(_dot(x1[i].astype(BF16), wxq_ref[...]) * (hd ** -0.5)).astype(BF16) for i in range(len(subs))]
    s = [[_dot_nt(q[i][:, c], km_ref[:, c]) for c in heads] for i in range(len(subs))]
    oh = []
    for i in range(len(subs)):
        row = []
        for h, c in enumerate(heads):
            m = jnp.max(s[i][h], axis=-1, keepdims=True)
            p = jnp.exp(s[i][h] - m)
            l = jnp.sum(p, axis=-1, keepdims=True)
            row.append((_dot(p.astype(BF16), vm_ref[:, c]) / l).astype(BF16))
        oh.append(row)
    for i, r in enumerate(subs):
        xa = sum(_dot(oh[i][h], wxo_ref[c, :]) for h, c in enumerate(heads))
        o_ref[0, r, :] = _layer_norm(alpha * x1[i] + xa, g2_ref[...], b2_ref[...])


def _post_mixer(x, moba_o, gla_o, w_mix_o, ln1_g, ln1_b, w_xq, mem, w_xkv, w_xo, ln2_g, ln2_b, alpha, tm=1024):
    B, S, D = x.shape
    aw = moba_o.shape[-1]
    M = mem.shape[1]
    wm = w_mix_o.astype(BF16)
    wxq = w_xq.astype(BF16)
    wxo = w_xo.astype(BF16)
    wkv = w_xkv.astype(BF16)
    vec = lambda a: a.reshape(1, D)
    row = lambda w: pl.BlockSpec((1, tm, w), lambda b, s: (b, s, 0))
    full = lambda a: pl.BlockSpec(a.shape, lambda b, s: (0,) * a.ndim)
    vspec = pl.BlockSpec((1, D), lambda b, s: (0, 0))
    memspec = pl.BlockSpec((1, M, D), lambda b, s: (b, 0, 0))
    return pl.pallas_call(
        functools.partial(_post_mixer_kernel, alpha),
        grid=(B, S // tm),
        in_specs=[row(D), row(aw), row(gla_o.shape[-1]), full(wm), vspec, vspec, full(wxq),
                  memspec, pl.BlockSpec(wkv.shape, lambda b, s: (0, 0), pipeline_mode=pl.Buffered(1)),
                  full(wxo), vspec, vspec],
        out_specs=row(D),
        out_shape=jax.ShapeDtypeStruct((B, S, D), F32),
        scratch_shapes=[pltpu.VMEM((M, D), BF16), pltpu.VMEM((M, D), BF16)],
        compiler_params=pltpu.CompilerParams(
            dimension_semantics=("arbitrary", "arbitrary"), vmem_limit_bytes=VMEM_LIMIT_BYTES),
        name="post_mixer",
    )(x, moba_o, gla_o, wm, vec(ln1_g), vec(ln1_b), wxq, mem, wkv, wxo, vec(ln2_g), vec(ln2_b))


def _mlp_kernel(alpha, fc, x_ref, w1_ref, w2_ref, g_ref, b_ref, o_ref):
    for r in range(x_ref.shape[1] // MLP_SUB_ROWS):
        rows = slice(r * MLP_SUB_ROWS, (r + 1) * MLP_SUB_ROWS)
        x = x_ref[0, rows, :]
        xb = x.astype(BF16)
        acc = jnp.zeros_like(x)
        for c in range(w1_ref.shape[1] // fc):
            cols = slice(c * fc, (c + 1) * fc)
            h = jnp.maximum(_dot(xb, w1_ref[:, cols]), 0.0)
            acc = acc + _dot((h * h).astype(BF16), w2_ref[cols, :])
        o_ref[0, rows, :] = _layer_norm(alpha * x + acc, g_ref[...], b_ref[...])


def _mlp(x, w_ff1, w_ff2, g, b, alpha, tm=1024, fc=512):
    B, S, D = x.shape
    w1 = w_ff1.astype(BF16)
    w2 = w_ff2.astype(BF16)
    row = pl.BlockSpec((1, tm, D), lambda bb, s: (bb, s, 0))
    full = lambda a: pl.BlockSpec(a.shape, lambda bb, s: (0, 0), pipeline_mode=pl.Buffered(1))
    vspec = pl.BlockSpec((1, D), lambda bb, s: (0, 0))
    return pl.pallas_call(
        functools.partial(_mlp_kernel, alpha, fc),
        grid=(B, S // tm),
        in_specs=[row, full(w1), full(w2), vspec, vspec],
        out_specs=row,
        out_shape=jax.ShapeDtypeStruct((B, S, D), F32),
        compiler_params=pltpu.CompilerParams(
            dimension_semantics=("arbitrary", "arbitrary"), vmem_limit_bytes=VMEM_LIMIT_BYTES),
        name="sq_relu_mlp",
    )(x, w1, w2, g.reshape(1, D), b.reshape(1, D))


def kernel(x, mem, w_in, w_gate_up, b_gate, gla_norm_g, w_mix_o, ln1_g, ln1_b, w_xq, w_xkv, w_xo,
           ln2_g, ln2_b, w_ff1, w_ff2, ln3_g, ln3_b):
    depth = w_in.shape[0]
    alpha = (2.0 * depth) ** 0.25
    for l in range(depth):
        qa, ka, vat, qg, kg, vg, rg, la = _in_projection(x, w_in[l], w_gate_up[l], b_gate[l])
        moba_o, (wkv, wmix, wxq, wxo, wf1, wf2) = _moba(
            qa, ka, vat, cast_weights=(w_xkv[l], w_mix_o[l], w_xq[l], w_xo[l], w_ff1[l], w_ff2[l]))
        gla_o = _gla(qg, kg, la, vg, rg, gla_norm_g[l].reshape(1, GLA_VAL_DIM))
        x = _post_mixer(x, moba_o, gla_o, wmix, ln1_g[l], ln1_b[l], wxq, mem, wkv, wxo, ln2_g[l], ln2_b[l], alpha)
        x = _mlp(x, wf1, wf2, ln3_g[l], ln3_b[l], alpha)
    return x
```

```python
import functools

import jax
import jax.numpy as jnp
import numpy as np
from jax import lax
from jax.experimental import pallas as pl
from jax.experimental.pallas import tpu as pltpu

F32 = jnp.float32
BF16 = jnp.bfloat16

MOBA_HEAD_DIM = 64
MOBA_VALUE_ROWS = 80
MOBA_BLOCK = 256
MOBA_TOPK = 3
GLA_HEADS = 4
GLA_KEY_DIM = 64
GLA_VAL_DIM = 128
GLA_GATE_RANK = 16
GLA_GATE_TEMP = 16.0
GATE_TERMS = 6
GLA_CHUNK = 64
GLA_SUB = 16
GLA_GROUP = 256
XATTN_HEADS = 4
POST_SUB_ROWS = 256
WEIGHT_CAST_TILE = 256
MLP_SUB_ROWS = 512
LN_EPS = 1e-5
RMS_EPS = 1e-6

LANES = 128
VMEM_LIMIT_BYTES = 56 * 1024 * 1024

NEG_INF = float("-inf")
LOG2E = 1.4426950408889634


def _dot(a, b):
    return jnp.dot(a, b, preferred_element_type=F32)


def _dot_nt(a, b):
    return lax.dot_general(a, b, (((1,), (1,)), ((), ())), preferred_element_type=F32)


def _dot_tn(a, b):
    return lax.dot_general(a, b, (((0,), (0,)), ((), ())), preferred_element_type=F32)


def _split3(a):
    hi = a.astype(BF16)
    r1 = a - hi.astype(F32)
    mid = r1.astype(BF16)
    lo = (r1 - mid.astype(F32)).astype(BF16)
    return hi, mid, lo


def _layer_norm(x, g, b):
    mu = jnp.mean(x, axis=-1, keepdims=True)
    xc = x - mu
    var = jnp.mean(xc * xc, axis=-1, keepdims=True)
    return xc * lax.rsqrt(var + LN_EPS) * g + b


def _log_sigmoid(z):
    return jnp.minimum(z, 0.0) - jnp.log(1.0 + jnp.exp(-jnp.abs(z)))


def _sigmoid(z):
    return 1.0 / (1.0 + jnp.exp(-z))


def _inproj_kernel(x_ref, wt_ref, wz_ref, bg_ref,
                   qa_ref, ka_ref, vat_ref, qg_ref, kg_ref, vg_ref, rg_ref, la_ref,
                   wqk_ref, wvat_ref, wg_ref, wvr_ref):
    aw, kw, vw = qa_ref.shape[-1], qg_ref.shape[-1], vg_ref.shape[-1]

    @pl.when((pl.program_id(0) == 0) & (pl.program_id(1) == 0))
    def _():
        def convert(dst, first_row):
            for r in range(0, dst.shape[0], WEIGHT_CAST_TILE):
                n = min(WEIGHT_CAST_TILE, dst.shape[0] - r)
                dst[r:r + n, :] = wt_ref[first_row + r:first_row + r + n, :].astype(BF16)

        convert(wqk_ref, 0)
        convert(wvat_ref, 2 * aw)
        convert(wg_ref.at[:2 * kw], 3 * aw)
        convert(wvr_ref, 3 * aw + 2 * kw)
        glr_row = 3 * aw + 2 * kw + 2 * vw
        glr = wt_ref[glr_row:glr_row + GLA_GATE_RANK, :].astype(BF16)
        for j in range(LANES // GLA_GATE_RANK):
            rows = slice(2 * kw + j * GLA_GATE_RANK, 2 * kw + (j + 1) * GLA_GATE_RANK)
            wg_ref[rows, :] = glr if j < GATE_TERMS else jnp.zeros_like(glr)

    xb = x_ref[0].astype(BF16)
    qk = _dot_nt(xb, wqk_ref[...])
    qa_ref[0] = (qk[:, :aw] * (MOBA_HEAD_DIM ** -0.5 * LOG2E)).astype(BF16)
    ka_ref[0] = qk[:, aw:].astype(BF16)
    vt = _dot_nt(wvat_ref[...], xb).astype(BF16)
    hd, va = MOBA_HEAD_DIM, MOBA_VALUE_ROWS
    for h in range(aw // hd):
        vat_ref[0, h * va:h * va + hd, :] = vt[h * hd:(h + 1) * hd, :]
        vat_ref[0, h * va + hd:(h + 1) * va, :] = jnp.ones((va - hd, vt.shape[1]), BF16)
    g = _dot_nt(xb, wg_ref[...])
    qg_ref[0] = g[:, :kw] * (GLA_KEY_DIM ** -0.5)
    kg_ref[0] = g[:, kw:2 * kw]
    vr = _dot_nt(xb, wvr_ref[...])
    vg_ref[0] = vr[:, :vw].astype(BF16)
    rg = vr[:, vw:]
    rg_ref[0] = rg * _sigmoid(rg)
    gh, gm, gl = _split3(g[:, 2 * kw:])
    grp = lax.broadcasted_iota(jnp.int32, (1, LANES), 1) // GLA_GATE_RANK
    lhs = jnp.where((grp == 1) | (grp == 4), gm, jnp.where(grp == 2, gl, gh))
    z = _dot(lhs, wz_ref[...]) + bg_ref[...]
    la_ref[0] = _log_sigmoid(z) / GLA_GATE_TEMP


def _in_projection(x, w_in, w_gate_up, b_gate, tm=1024):
    B, S, D = x.shape
    aw = 512
    vrows = aw // MOBA_HEAD_DIM * MOBA_VALUE_ROWS
    kw = GLA_HEADS * GLA_KEY_DIM
    vw = GLA_HEADS * GLA_VAL_DIM
    wt = jnp.swapaxes(w_in, 0, 1)
    wh, wm, wl = _split3(w_gate_up)
    pad = LANES - GATE_TERMS * GLA_GATE_RANK
    wz = jnp.pad(jnp.concatenate([wh, wh, wh, wm, wm, wl], axis=0), ((0, pad), (0, 0)))
    bg = b_gate.reshape(1, kw)

    def full(a):
        return pl.BlockSpec(a.shape, lambda b, s: (0,) * a.ndim)

    row = lambda w: pl.BlockSpec((1, tm, w), lambda b, s: (b, s, 0))
    out_shape = (
        jax.ShapeDtypeStruct((B, S, aw), BF16),
        jax.ShapeDtypeStruct((B, S, aw), BF16),
        jax.ShapeDtypeStruct((B, vrows, S), BF16),
        jax.ShapeDtypeStruct((B, S, kw), F32),
        jax.ShapeDtypeStruct((B, S, kw), F32),
        jax.ShapeDtypeStruct((B, S, vw), BF16),
        jax.ShapeDtypeStruct((B, S, vw), F32),
        jax.ShapeDtypeStruct((B, S, kw), F32),
    )
    out_specs = (row(aw), row(aw), pl.BlockSpec((1, vrows, tm), lambda b, s: (b, 0, s)),
                 row(kw), row(kw), row(vw), row(vw), row(kw))
    return pl.pallas_call(
        _inproj_kernel,
        grid=(B, S // tm),
        in_specs=[row(D), pl.BlockSpec(wt.shape, lambda b, s: (0, 0), pipeline_mode=pl.Buffered(1)),
                  full(wz), full(bg)],
        out_specs=out_specs,
        out_shape=out_shape,
        scratch_shapes=[
            pltpu.VMEM((2 * aw, D), BF16),
            pltpu.VMEM((aw, D), BF16),
            pltpu.VMEM((2 * kw + LANES, D), BF16),
            pltpu.VMEM((2 * vw, D), BF16),
        ],
        compiler_params=pltpu.CompilerParams(
            dimension_semantics=("arbitrary", "arbitrary"), vmem_limit_bytes=VMEM_LIMIT_BYTES),
        name="in_projection",
    )(x, wt, wz, bg)


def _alibi_tables(n_heads, bs):
    def split3(v):
        pieces = []
        for _ in range(3):
            piece = v.astype(BF16)
            pieces.append(piece)
            v = v - piece.astype(np.float64)
        return pieces

    slopes = 2.0 ** (-8.0 * np.arange(1, n_heads + 1, dtype=np.float64) / n_heads)
    sig = slopes * LOG2E
    rel = np.arange(bs, dtype=np.float64)
    a = -sig[:, None] * rel[None, :]
    cols = np.stack(split3(a) + split3(np.broadcast_to(sig[:, None], a.shape)), axis=-1)
    qaux = np.concatenate([cols[0::2], cols[1::2]], axis=-1)
    qaux = np.pad(qaux, ((0, 0), (0, 0), (0, LANES - qaux.shape[-1])))
    one = np.ones((bs,), np.float64)
    kcols = np.stack([one, one, one, rel, rel, rel], axis=-1)
    zeros = np.zeros_like(kcols)
    kaux = np.stack([np.concatenate([kcols, zeros], axis=-1), np.concatenate([zeros, kcols], axis=-1)])
    kaux = np.pad(kaux, ((0, 0), (0, 0), (0, LANES - kaux.shape[-1]))).astype(BF16)
    return jnp.asarray(sig, F32), jnp.asarray(qaux), jnp.asarray(kaux)


def _moba_kernel(n_cast, sig_ref, q_ref, k_ref, vt_ref, qaux_ref, kaux_ref, *refs):
    cast_in, (o_ref, *cast_out), scratch = refs[:n_cast], refs[n_cast:2 * n_cast + 1], refs[2 * n_cast + 1:]
    kmean_ref, kaug_ref, kms_ref, sel_ref, acc_ref, m_ref, s_ref, mx_ref = scratch
    for src, dst in zip(cast_in, cast_out):
        dst[...] = src[...].astype(BF16)
    hg = pl.program_id(1)
    i = pl.program_id(2)
    bs = MOBA_BLOCK
    nb = k_ref.shape[1] // bs
    hd = MOBA_HEAD_DIM
    va = MOBA_VALUE_ROWS
    n_heads = q_ref.shape[-1] // hd
    n_pairs = n_heads // 2
    pair_lanes = [slice(p * LANES, (p + 1) * LANES) for p in range(n_pairs)]
    gate_rows = 3 * nb

    @pl.when(i == 0)
    def _():
        lane = lax.broadcasted_iota(jnp.int32, (1, LANES), 1)
        first = lane < hd
        for n in range(nb):
            kb = k_ref[0, n * bs:(n + 1) * bs, :]
            kmean_ref[n:n + 1, :] = jnp.sum(kb.astype(F32), axis=0, keepdims=True) * (1.0 / bs)
            for p in range(n_pairs):
                kp = kb[:, pair_lanes[p]]
                zero = jnp.zeros_like(kp)
                for h, kh in enumerate((jnp.where(first, kp, zero), jnp.where(first, zero, kp))):
                    rows = slice((2 * n + h) * bs, (2 * n + h + 1) * bs)
                    kaug_ref[p, rows, :] = jnp.concatenate([kh, kaux_ref[h]], axis=1)
        pieces = _split3(kmean_ref[...])
        for p in range(n_pairs):
            for h in range(2):
                for r, piece in enumerate(pieces):
                    pp = piece[:, pair_lanes[p]]
                    zero = jnp.zeros_like(pp)
                    ph = jnp.where(first, pp, zero) if h == 0 else jnp.where(first, zero, pp)
                    rows = slice(h * gate_rows + r * nb, h * gate_rows + (r + 1) * nb)
                    kms_ref[p, rows, :] = jnp.concatenate([ph, zero], axis=1)

    qaug = [jnp.concatenate([q_ref[0, :, pair_lanes[p]], qaux_ref[p]], axis=1) for p in range(n_pairs)]
    sig = [sig_ref[hg * n_heads + h] for h in range(n_heads)]

    def select_blocks(h, g):
        nidx = lax.broadcasted_iota(jnp.int32, (nb, bs), 0)
        g = jnp.where(nidx < i, g, NEG_INF)
        sel = jnp.zeros((nb, bs), jnp.bool_)
        for _ in range(MOBA_TOPK):
            best = jnp.max(g, axis=0, keepdims=True)
            first = jnp.min(jnp.where(g == best, nidx, nb), axis=0, keepdims=True)
            pick = nidx == first
            sel = sel | pick
            g = jnp.where(pick, NEG_INF, g)
        sel_ref[h] = jnp.where(sel & (nidx < i), 0.0, NEG_INF)

    def stage_a(blk, slot, p, with_gate=False):
        start = pl.multiple_of(blk * (2 * bs), 2 * bs)
        lhs = kaug_ref[p, pl.ds(start, 2 * bs), :]
        if with_gate:
            lhs = jnp.concatenate([lhs, kms_ref[p]], axis=0)
        t = _dot_nt(lhs, qaug[p])
        for h in range(2):
            if with_gate:
                g0 = 2 * bs + h * gate_rows
                select_blocks(2 * p + h, t[g0:g0 + nb] + t[g0 + nb:g0 + 2 * nb] + t[g0 + 2 * nb:g0 + 3 * nb])
            th = t[h * bs:(h + 1) * bs]
            s_ref[slot, 2 * p + h] = th
            mx_ref[slot, 2 * p + h:2 * p + h + 1, :] = jnp.max(th, axis=0, keepdims=True)

    def values(blk, h):
        start = pl.multiple_of(blk * bs, bs)
        return vt_ref[0, h * va:(h + 1) * va, pl.ds(start, bs)]

    def stage_b_own(slot, p):
        krow = lax.broadcasted_iota(jnp.int32, (bs, bs), 0)
        qcol = lax.broadcasted_iota(jnp.int32, (bs, bs), 1)
        causal = qcol >= krow
        for h in (2 * p, 2 * p + 1):
            t = jnp.where(causal, s_ref[slot, h], NEG_INF)
            m = jnp.max(t, axis=0, keepdims=True)
            pr = jnp.exp2(t - m)
            m_ref[h:h + 1, :] = m
            acc_ref[h] = _dot(values(i, h), pr.astype(BF16))

    def stage_b_past(blk, slot, p):
        off = ((i - blk) * bs).astype(F32)
        for h in (2 * p, 2 * p + 1):
            cs = sel_ref[h, pl.ds(blk, 1), :] - sig[h] * off
            m_old = m_ref[h:h + 1, :]
            m_new = jnp.maximum(m_old, mx_ref[slot, h:h + 1, :] + cs)
            alpha = jnp.exp2(m_old - m_new)
            pr = jnp.exp2(s_ref[slot, h] + (cs - m_new))
            m_ref[h:h + 1, :] = m_new
            acc_ref[h] = alpha * acc_ref[h] + _dot(values(blk, h), pr.astype(BF16))

    for p in range(n_pairs):
        stage_a(i, 0, p, with_gate=True)
    for p in range(n_pairs):
        stage_a(jnp.maximum(i - 1, 0), 1, p)
        stage_b_own(0, p)

    def step(u, carry):
        blk = i - 1 - 2 * u
        for p in range(n_pairs):
            stage_a(blk - 1, 0, p)
            stage_b_past(blk, 1, p)
        for p in range(n_pairs):
            stage_a(jnp.maximum(blk - 2, 0), 1, p)
            stage_b_past(blk - 1, 0, p)
        return carry

    lax.fori_loop(0, i // 2, step, 0)

    @pl.when(i % 2 == 1)
    def _():
        for p in range(n_pairs):
            stage_b_past(0, 1, p)

    for p in range(n_pairs):
        ot = jnp.concatenate([acc_ref[2 * p + h, :hd, :] / acc_ref[2 * p + h, hd:hd + 1, :] for h in range(2)], axis=0)
        o_ref[0, :, pair_lanes[p]] = ot.T.astype(BF16)


def _moba(qa, ka, vat, cast_weights=(), heads_per_step=8):
    B, S, W = qa.shape
    bs = MOBA_BLOCK
    nb = S // bs
    n_heads = W // MOBA_HEAD_DIM
    gw = heads_per_step * MOBA_HEAD_DIM
    sig, qaux, kaux = _alibi_tables(n_heads, bs)
    n_steps = B * (W // gw) * nb
    cast_specs = [pl.BlockSpec((w.shape[0] // n_steps, w.shape[1]), lambda b, hg, i: ((b * (W // gw) + hg) * nb + i, 0))
                  for w in cast_weights]
    outs = pl.pallas_call(
        functools.partial(_moba_kernel, len(cast_weights)),
        grid=(B, W // gw, nb),
        in_specs=[
            pl.BlockSpec(memory_space=pltpu.SMEM),
            pl.BlockSpec((1, bs, gw), lambda b, hg, i: (b, i, hg)),
            pl.BlockSpec((1, S, gw), lambda b, hg, i: (b, 0, hg)),
            pl.BlockSpec((1, heads_per_step * MOBA_VALUE_ROWS, S), lambda b, hg, i: (b, hg, 0)),
            pl.BlockSpec((heads_per_step // 2, bs, LANES), lambda b, hg, i: (hg, 0, 0)),
            pl.BlockSpec((2, bs, LANES), lambda b, hg, i: (0, 0, 0)),
        ] + cast_specs,
        out_specs=[pl.BlockSpec((1, bs, gw), lambda b, hg, i: (b, i, hg))] + cast_specs,
        out_shape=[jax.ShapeDtypeStruct((B, S, W), BF16)] + [jax.ShapeDtypeStruct(w.shape, BF16) for w in cast_weights],
        scratch_shapes=[
            pltpu.VMEM((nb, gw), F32),
            pltpu.VMEM((heads_per_step // 2, 2 * S, 2 * LANES), BF16),
            pltpu.VMEM((heads_per_step // 2, 6 * nb, 2 * LANES), BF16),
            pltpu.VMEM((heads_per_step, nb, bs), F32),
            pltpu.VMEM((heads_per_step, MOBA_VALUE_ROWS, bs), F32),
            pltpu.VMEM((8, bs), F32),
            pltpu.VMEM((2, heads_per_step, bs, bs), F32),
            pltpu.VMEM((2, 8, bs), F32),
        ],
        compiler_params=pltpu.CompilerParams(
            dimension_semantics=("arbitrary", "arbitrary", "arbitrary"), vmem_limit_bytes=VMEM_LIMIT_BYTES),
        name="moba_attention",
    )(sig, qa, ka, vat, qaux, kaux, *cast_weights)
    return outs[0], outs[1:]


def _gla_kernel(q_ref, k_ref, la_ref, v_ref, r_ref, gn_ref, o_ref, st_ref):
    s_idx = pl.program_id(1)
    C = GLA_CHUNK
    tc = q_ref.shape[1]

    @pl.when(s_idx == 0)
    def _():
        st_ref[...] = jnp.zeros_like(st_ref)

    G = GLA_GROUP
    n_pairs = GLA_HEADS // 2
    lane = lax.broadcasted_iota(jnp.int32, (1, LANES), 1)
    head_lanes = (lane < GLA_KEY_DIM, lane >= GLA_KEY_DIM)
    ti = lax.broadcasted_iota(jnp.int32, (G, G), 0)
    si = lax.broadcasted_iota(jnp.int32, (G, G), 1)
    causal = (ti // C == si // C) & (si <= ti)
    tril = jnp.where(causal, 1.0, 0.0).astype(BF16)
    gn = gn_ref[...]

    cpg = G // C
    n_groups = tc // G
    grp_rows = [slice(g * G, (g + 1) * G) for g in range(n_groups)]
    chunk_rows = [slice(c * C, (c + 1) * C) for c in range(cpg)]
    pair_lanes = [slice(hp * LANES, (hp + 1) * LANES) for hp in range(n_pairs)]
    val_lanes = [slice(hd * GLA_VAL_DIM, (hd + 1) * GLA_VAL_DIM) for hd in range(GLA_HEADS)]

    b = [sum(_dot(tril, piece) for piece in _split3(la_ref[0, rows, :])) for rows in grp_rows]

    sub = GLA_SUB
    spc = C // sub
    kw = q_ref.shape[-1]
    sub_of_row = (lax.broadcasted_iota(jnp.int32, (G, 1), 0) % C) // sub
    q_m, q_sub, k_sub, k_dec, decay = [], [], [], [], []
    for g, rows in enumerate(grp_rows):
        bg = b[g]
        b_last = [bg[c * C + C - 1:c * C + C, :] for c in range(cpg)]
        b_end = jnp.concatenate([jnp.broadcast_to(bl, (C, kw)) for bl in b_last], axis=0)
        zero_row = jnp.zeros_like(b_last[0])
        ref = [[zero_row if v == 0 else bg[c * C + v * sub - 1:c * C + v * sub, :] for v in range(spc)]
               for c in range(cpg)]
        ref_own = jnp.concatenate([jnp.broadcast_to(ref[c][v], (sub, kw)) for c in range(cpg) for v in range(spc)],
                                  axis=0)
        qs = q_ref[0, rows, :]
        ks = k_ref[0, rows, :]
        q_dec = qs * jnp.exp(bg)
        q_in = qs * jnp.exp(bg - ref_own)
        k_variants = []
        for v in range(spc):
            ref_v = jnp.concatenate([jnp.broadcast_to(ref[c][v], (C, kw)) for c in range(cpg)], axis=0)
            k_variants.append(jnp.where(sub_of_row <= v, ks * jnp.exp(ref_v - bg), 0.0).astype(BF16))
        k_sub.append(k_variants)
        k_dec.append((ks * jnp.exp(b_end - bg)).astype(BF16))
        decay.append([jnp.exp(bl) for bl in b_last])
        q_m.append([jnp.where(head_lanes[hd % 2], q_dec[:, pair_lanes[hd // 2]], 0.0).astype(BF16)
                    for hd in range(GLA_HEADS)])
        q_sub.append([[jnp.where(head_lanes[hd % 2] & (sub_of_row == v), q_in[:, pair_lanes[hd // 2]], 0.0).astype(BF16)
                       for v in range(spc)] for hd in range(GLA_HEADS)])

    def intra_scores(g, hd):
        lanes = pair_lanes[hd // 2]
        return sum(_dot_nt(jnp.concatenate([q_sub[g][hd][v], q_sub[g][hd][v + 1]], axis=1),
                           jnp.concatenate([k_sub[g][v][:, lanes], k_sub[g][v + 1][:, lanes]], axis=1))
                   for v in range(0, spc, 2))

    attn = [[intra_scores(g, hd) for hd in range(GLA_HEADS)]
            for g in range(n_groups)]

    update = [[[None] * cpg for _ in range(n_pairs)] for _ in range(n_groups)]
    for g, rows in enumerate(grp_rows):
        for hp in range(n_pairs):
            v_pair = v_ref[0, rows, hp * 2 * GLA_VAL_DIM:(hp + 1) * 2 * GLA_VAL_DIM]
            for c, cr in enumerate(chunk_rows):
                u = _dot_tn(v_pair[cr], k_dec[g][cr, pair_lanes[hp]])
                update[g][hp][c] = jnp.where(head_lanes[0], u[:GLA_VAL_DIM], u[GLA_VAL_DIM:])

    o_intra = [[_dot(jnp.where(causal, attn[g][hd], 0.0).astype(BF16), v_ref[0, grp_rows[g], val_lanes[hd]])
                for hd in range(GLA_HEADS)] for g in range(n_groups)]

    states = [st_ref[hp] for hp in range(n_pairs)]
    for g, rows in enumerate(grp_rows):
        o_inter = [[None] * cpg for _ in range(GLA_HEADS)]
        for c, cr in enumerate(chunk_rows):
            for hp in range(n_pairs):
                q_pair = jnp.concatenate([q_m[g][2 * hp][cr], q_m[g][2 * hp + 1][cr]], axis=0)
                oi = _dot_nt(q_pair, states[hp].astype(BF16))
                o_inter[2 * hp][c] = oi[:C]
                o_inter[2 * hp + 1][c] = oi[C:]
                states[hp] = states[hp] * decay[g][c][:, pair_lanes[hp]] + update[g][hp][c]
        for hd in range(GLA_HEADS):
            o = jnp.concatenate(o_inter[hd], axis=0) + o_intra[g][hd]
            ms = jnp.mean(o * o, axis=-1, keepdims=True)
            o_ref[0, rows, val_lanes[hd]] = (o * lax.rsqrt(ms + RMS_EPS) * gn * r_ref[0, rows, val_lanes[hd]]).astype(BF16)

    for hp in range(n_pairs):
        st_ref[hp] = states[hp]


def _gla(qg, kg, la, vg, rg, gnorm, tc=1024):
    B, S, kw = qg.shape
    vw = vg.shape[-1]
    row = lambda w: pl.BlockSpec((1, tc, w), lambda b, s: (b, s, 0))
    return pl.pallas_call(
        _gla_kernel,
        grid=(B, S // tc),
        in_specs=[row(kw), row(kw), row(kw), row(vw), row(vw),
                  pl.BlockSpec((1, GLA_VAL_DIM), lambda b, s: (0, 0))],
        out_specs=row(vw),
        out_shape=jax.ShapeDtypeStruct((B, S, vw), BF16),
        scratch_shapes=[pltpu.VMEM((GLA_HEADS // 2, GLA_VAL_DIM, LANES), F32)],
        compiler_params=pltpu.CompilerParams(
            dimension_semantics=("arbitrary", "arbitrary"), vmem_limit_bytes=VMEM_LIMIT_BYTES),
        name="gla_scan",
    )(qg, kg, la, vg, rg, gnorm)


def _post_mixer_kernel(alpha, x_ref, mo_ref, go_ref, wm_ref, g1_ref, b1_ref, wxq_ref,
                       mem_ref, wkv_ref, wxo_ref, g2_ref, b2_ref, o_ref, km_ref, vm_ref):
    @pl.when(pl.program_id(1) == 0)
    def _():
        kv = _dot(mem_ref[0].astype(BF16), wkv_ref[...])
        km_ref[...] = kv[:, :km_ref.shape[-1]].astype(BF16)
        vm_ref[...] = kv[:, km_ref.shape[-1]:].astype(BF16)

    tm, d = x_ref.shape[1], x_ref.shape[2]
    hd = d // XATTN_HEADS
    subs = [slice(r * POST_SUB_ROWS, (r + 1) * POST_SUB_ROWS) for r in range(tm // POST_SUB_ROWS)]
    heads = [slice(h * hd, (h + 1) * hd) for h in range(XATTN_HEADS)]
    aw = mo_ref.shape[-1]
    mix = [_dot(mo_ref[0, r, :], wm_ref[:aw, :]) + _dot(go_ref[0, r, :], wm_ref[aw:, :]) for r in subs]
    x1 = [_layer_norm(alpha * x_ref[0, r, :] + mix[i], g1_ref[...], b1_ref[...]) for i, r in enumerate(subs)]
    q = [(_dot(x1[i].astype(BF16), wxq_ref[...]) * (hd ** -0.5)).astype(BF16) for i in range(len(subs))]
    s = [[_dot_nt(q[i][:, c], km_ref[:, c]) for c in heads] for i in range(len(subs))]
    oh = []
    for i in range(len(subs)):
        row = []
        for h, c in enumerate(heads):
            m = jnp.max(s[i][h], axis=-1, keepdims=True)
            p = jnp.exp(s[i][h] - m)
            l = jnp.sum(p, axis=-1, keepdims=True)
            row.append((_dot(p.astype(BF16), vm_ref[:, c]) / l).astype(BF16))
        oh.append(row)
    for i, r in enumerate(subs):
        xa = sum(_dot(oh[i][h], wxo_ref[c, :]) for h, c in enumerate(heads))
        o_ref[0, r, :] = _layer_norm(alpha * x1[i] + xa, g2_ref[...], b2_ref[...])


def _post_mixer(x, moba_o, gla_o, w_mix_o, ln1_g, ln1_b, w_xq, mem, w_xkv, w_xo, ln2_g, ln2_b, alpha, tm=1024):
    B, S, D = x.shape
    aw = moba_o.shape[-1]
    M = mem.shape[1]
    wm = w_mix_o.astype(BF16)
    wxq = w_xq.astype(BF16)
    wxo = w_xo.astype(BF16)
    wkv = w_xkv.astype(BF16)
    vec = lambda a: a.reshape(1, D)
    row = lambda w: pl.BlockSpec((1, tm, w), lambda b, s: (b, s, 0))
    full = lambda a: pl.BlockSpec(a.shape, lambda b, s: (0,) * a.ndim)
    vspec = pl.BlockSpec((1, D), lambda b, s: (0, 0))
    memspec = pl.BlockSpec((1, M, D), lambda b, s: (b, 0, 0))
    return pl.pallas_call(
        functools.partial(_post_mixer_kernel, alpha),
        grid=(B, S // tm),
        in_specs=[row(D), row(aw), row(gla_o.shape[-1]), full(wm), vspec, vspec, full(wxq),
                  memspec, pl.BlockSpec(wkv.shape, lambda b, s: (0, 0), pipeline_mode=pl.Buffered(1)),
                  full(wxo), vspec, vspec],
        out_specs=row(D),
        out_shape=jax.ShapeDtypeStruct((B, S, D), F32),
        scratch_shapes=[pltpu.VMEM((M, D), BF16), pltpu.VMEM((M, D), BF16)],
        compiler_params=pltpu.CompilerParams(
            dimension_semantics=("arbitrary", "arbitrary"), vmem_limit_bytes=VMEM_LIMIT_BYTES),
        name="post_mixer",
    )(x, moba_o, gla_o, wm, vec(ln1_g), vec(ln1_b), wxq, mem, wkv, wxo, vec(ln2_g), vec(ln2_b))


def _mlp_kernel(alpha, fc, x_ref, w1_ref, w2_ref, g_ref, b_ref, o_ref):
    for r in range(x_ref.shape[1] // MLP_SUB_ROWS):
        rows = slice(r * MLP_SUB_ROWS, (r + 1) * MLP_SUB_ROWS)
        x = x_ref[0, rows, :]
        xb = x.astype(BF16)
        acc = jnp.zeros_like(x)
        for c in range(w1_ref.shape[1] // fc):
            cols = slice(c * fc, (c + 1) * fc)
            h = jnp.maximum(_dot(xb, w1_ref[:, cols]), 0.0)
            acc = acc + _dot((h * h).astype(BF16), w2_ref[cols, :])
        o_ref[0, rows, :] = _layer_norm(alpha * x + acc, g_ref[...], b_ref[...])


def _mlp(x, w_ff1, w_ff2, g, b, alpha, tm=1024, fc=512):
    B, S, D = x.shape
    w1 = w_ff1.astype(BF16)
    w2 = w_ff2.astype(BF16)
    row = pl.BlockSpec((1, tm, D), lambda bb, s: (bb, s, 0))
    full = lambda a: pl.BlockSpec(a.shape, lambda bb, s: (0, 0), pipeline_mode=pl.Buffered(1))
    vspec = pl.BlockSpec((1, D), lambda bb, s: (0, 0))
    return pl.pallas_call(
        functools.partial(_mlp_kernel, alpha, fc),
        grid=(B, S // tm),
        in_specs=[row, full(w1), full(w2), vspec, vspec],
        out_specs=row,
        out_shape=jax.ShapeDtypeStruct((B, S, D), F32),
        compiler_params=pltpu.CompilerParams(
            dimension_semantics=("arbitrary", "arbitrary"), vmem_limit_bytes=VMEM_LIMIT_BYTES),
        name="sq_relu_mlp",
    )(x, w1, w2, g.reshape(1, D), b.reshape(1, D))


def kernel(x, mem, w_in, w_gate_up, b_gate, gla_norm_g, w_mix_o, ln1_g, ln1_b, w_xq, w_xkv, w_xo,
           ln2_g, ln2_b, w_ff1, w_ff2, ln3_g, ln3_b):
    depth = w_in.shape[0]
    alpha = (2.0 * depth) ** 0.25
    for l in range(depth):
        qa, ka, vat, qg, kg, vg, rg, la = _in_projection(x, w_in[l], w_gate_up[l], b_gate[l])
        moba_o, (wkv, wmix, wxq, wxo, wf1, wf2) = _moba(
            qa, ka, vat, cast_weights=(w_xkv[l], w_mix_o[l], w_xq[l], w_xo[l], w_ff1[l], w_ff2[l]))
        gla_o = _gla(qg, kg, la, vg, rg, gla_norm_g[l].reshape(1, GLA_VAL_DIM))
        x = _post_mixer(x, moba_o, gla_o, wmix, ln1_g[l], ln1_b[l], wxq, mem, wkv, wxo, ln2_g[l], ln2_b[l], alpha)
        x = _mlp(x, wf1, wf2, ln3_g[l], ln3_b[l], alpha)
    return x
```

```python
import functools

import jax
import jax.numpy as jnp
import numpy as np
from jax import lax
from jax.experimental import pallas as pl
from jax.experimental.pallas import tpu as pltpu

F32 = jnp.float32
BF16 = jnp.bfloat16

MOBA_HEAD_DIM = 64
MOBA_VALUE_ROWS = 80
MOBA_BLOCK = 256
MOBA_TOPK = 3
GLA_HEADS = 4
GLA_KEY_DIM = 64
GLA_VAL_DIM = 128
GLA_GATE_RANK = 16
GLA_GATE_TEMP = 16.0
GATE_TERMS = 6
GLA_CHUNK = 64
GLA_SUB = 16
GLA_GROUP = 256
XATTN_HEADS = 4
POST_SUB_ROWS = 256
WEIGHT_CAST_TILE = 256
MLP_SUB_ROWS = 512
LN_EPS = 1e-5
RMS_EPS = 1e-6

LANES = 128
VMEM_LIMIT_BYTES = 56 * 1024 * 1024

NEG_INF = float("-inf")
LOG2E = 1.4426950408889634


def _dot(a, b):
    return jnp.dot(a, b, preferred_element_type=F32)


def _dot_nt(a, b):
    return lax.dot_general(a, b, (((1,), (1,)), ((), ())), preferred_element_type=F32)


def _dot_tn(a, b):
    return lax.dot_general(a, b, (((0,), (0,)), ((), ())), preferred_element_type=F32)


def _split3(a):
    hi = a.astype(BF16)
    r1 = a - hi.astype(F32)
    mid = r1.astype(BF16)
    lo = (r1 - mid.astype(F32)).astype(BF16)
    return hi, mid, lo


def _layer_norm(x, g, b):
    mu = jnp.mean(x, axis=-1, keepdims=True)
    xc = x - mu
    var = jnp.mean(xc * xc, axis=-1, keepdims=True)
    return xc * lax.rsqrt(var + LN_EPS) * g + b


def _log_sigmoid(z):
    return jnp.minimum(z, 0.0) - jnp.log(1.0 + jnp.exp(-jnp.abs(z)))


def _sigmoid(z):
    return 1.0 / (1.0 + jnp.exp(-z))


def _inproj_kernel(x_ref, wt_ref, wgu_ref, bg_ref,
                   qa_ref, ka_ref, vat_ref, qg_ref, kg_ref, vg_ref, rg_ref, la_ref,
                   wqk_ref, wvat_ref, wg_ref, wvr_ref, wz_ref):
    aw, kw, vw = qa_ref.shape[-1], qg_ref.shape[-1], vg_ref.shape[-1]

    @pl.when((pl.program_id(0) == 0) & (pl.program_id(1) == 0))
    def _():
        def convert(dst, first_row):
            for r in range(0, dst.shape[0], WEIGHT_CAST_TILE):
                n = min(WEIGHT_CAST_TILE, dst.shape[0] - r)
                dst[r:r + n, :] = wt_ref[first_row + r:first_row + r + n, :].astype(BF16)

        convert(wqk_ref, 0)
        convert(wvat_ref, 2 * aw)
        convert(wg_ref.at[:2 * kw], 3 * aw)
        convert(wvr_ref, 3 * aw + 2 * kw)
        glr_row = 3 * aw + 2 * kw + 2 * vw
        glr = wt_ref[glr_row:glr_row + GLA_GATE_RANK, :].astype(BF16)
        for j in range(LANES // GLA_GATE_RANK):
            rows = slice(2 * kw + j * GLA_GATE_RANK, 2 * kw + (j + 1) * GLA_GATE_RANK)
            wg_ref[rows, :] = glr if j < GATE_TERMS else jnp.zeros_like(glr)
        wh, wm, wl = _split3(wgu_ref[...])
        for j, piece in enumerate((wh, wh, wh, wm, wm, wl, jnp.zeros_like(wh), jnp.zeros_like(wh))):
            wz_ref[j * GLA_GATE_RANK:(j + 1) * GLA_GATE_RANK, :] = piece

    xb = x_ref[0].astype(BF16)
    qk = _dot_nt(xb, wqk_ref[...])
    qa_ref[0] = (qk[:, :aw] * (MOBA_HEAD_DIM ** -0.5 * LOG2E)).astype(BF16)
    ka_ref[0] = qk[:, aw:].astype(BF16)
    vt = _dot_nt(wvat_ref[...], xb).astype(BF16)
    hd, va = MOBA_HEAD_DIM, MOBA_VALUE_ROWS
    for h in range(aw // hd):
        vat_ref[0, h * va:h * va + hd, :] = vt[h * hd:(h + 1) * hd, :]
        vat_ref[0, h * va + hd:(h + 1) * va, :] = jnp.ones((va - hd, vt.shape[1]), BF16)
    g = _dot_nt(xb, wg_ref[...])
    qg_ref[0] = g[:, :kw] * (GLA_KEY_DIM ** -0.5)
    kg_ref[0] = g[:, kw:2 * kw]
    vr = _dot_nt(xb, wvr_ref[...])
    vg_ref[0] = vr[:, :vw].astype(BF16)
    rg = vr[:, vw:]
    rg_ref[0] = rg * _sigmoid(rg)
    gh, gm, gl = _split3(g[:, 2 * kw:])
    grp = lax.broadcasted_iota(jnp.int32, (1, LANES), 1) // GLA_GATE_RANK
    lhs = jnp.where((grp == 1) | (grp == 4), gm, jnp.where(grp == 2, gl, gh))
    z = _dot(lhs, wz_ref[...]) + bg_ref[...]
    la_ref[0] = _log_sigmoid(z) / GLA_GATE_TEMP


def _in_projection(x, w_in, w_gate_up, b_gate, tm=1024):
    B, S, D = x.shape
    aw = 512
    vrows = aw // MOBA_HEAD_DIM * MOBA_VALUE_ROWS
    kw = GLA_HEADS * GLA_KEY_DIM
    vw = GLA_HEADS * GLA_VAL_DIM
    wt = jnp.swapaxes(w_in, 0, 1)
    bg = b_gate.reshape(1, kw)

    def full(a):
        return pl.BlockSpec(a.shape, lambda b, s: (0,) * a.ndim)

    row = lambda w: pl.BlockSpec((1, tm, w), lambda b, s: (b, s, 0))
    out_shape = (
        jax.ShapeDtypeStruct((B, S, aw), BF16),
        jax.ShapeDtypeStruct((B, S, aw), BF16),
        jax.ShapeDtypeStruct((B, vrows, S), BF16),
        jax.ShapeDtypeStruct((B, S, kw), F32),
        jax.ShapeDtypeStruct((B, S, kw), F32),
        jax.ShapeDtypeStruct((B, S, vw), BF16),
        jax.ShapeDtypeStruct((B, S, vw), F32),
        jax.ShapeDtypeStruct((B, S, kw), F32),
    )
    out_specs = (row(aw), row(aw), pl.BlockSpec((1, vrows, tm), lambda b, s: (b, 0, s)),
                 row(kw), row(kw), row(vw), row(vw), row(kw))
    return pl.pallas_call(
        _inproj_kernel,
        grid=(B, S // tm),
        in_specs=[row(D), pl.BlockSpec(wt.shape, lambda b, s: (0, 0), pipeline_mode=pl.Buffered(1)),
                  full(w_gate_up), full(bg)],
        out_specs=out_specs,
        out_shape=out_shape,
        scratch_shapes=[
            pltpu.VMEM((2 * aw, D), BF16),
            pltpu.VMEM((aw, D), BF16),
            pltpu.VMEM((2 * kw + LANES, D), BF16),
            pltpu.VMEM((2 * vw, D), BF16),
            pltpu.VMEM((LANES, kw), BF16),
        ],
        compiler_params=pltpu.CompilerParams(
            dimension_semantics=("arbitrary", "arbitrary"), vmem_limit_bytes=VMEM_LIMIT_BYTES),
        name="in_projection",
    )(x, wt, w_gate_up, bg)


def _alibi_tables(n_heads, bs):
    def split3(v):
        pieces = []
        for _ in range(3):
            piece = v.astype(BF16)
            pieces.append(piece)
            v = v - piece.astype(np.float64)
        return pieces

    slopes = 2.0 ** (-8.0 * np.arange(1, n_heads + 1, dtype=np.float64) / n_heads)
    sig = slopes * LOG2E
    rel = np.arange(bs, dtype=np.float64)
    a = -sig[:, None] * rel[None, :]
    cols = np.stack(split3(a) + split3(np.broadcast_to(sig[:, None], a.shape)), axis=-1)
    qaux = np.concatenate([cols[0::2], cols[1::2]], axis=-1)
    qaux = np.pad(qaux, ((0, 0), (0, 0), (0, LANES - qaux.shape[-1])))
    one = np.ones((bs,), np.float64)
    kcols = np.stack([one, one, one, rel, rel, rel], axis=-1)
    zeros = np.zeros_like(kcols)
    kaux = np.stack([np.concatenate([kcols, zeros], axis=-1), np.concatenate([zeros, kcols], axis=-1)])
    kaux = np.pad(kaux, ((0, 0), (0, 0), (0, LANES - kaux.shape[-1]))).astype(BF16)
    return jnp.asarray(sig, F32), jnp.asarray(qaux), jnp.asarray(kaux)


def _moba_kernel(n_cast, sig_ref, q_ref, k_ref, vt_ref, qaux_ref, kaux_ref, *refs):
    cast_in, (o_ref, *cast_out), scratch = refs[:n_cast], refs[n_cast:2 * n_cast + 1], refs[2 * n_cast + 1:]
    kmean_ref, kaug_ref, kms_ref, sel_ref, acc_ref, m_ref, s_ref, mx_ref = scratch
    for src, dst in zip(cast_in, cast_out):
        dst[...] = src[...].astype(BF16)
    hg = pl.program_id(1)
    i = pl.program_id(2)
    bs = MOBA_BLOCK
    nb = k_ref.shape[1] // bs
    hd = MOBA_HEAD_DIM
    va = MOBA_VALUE_ROWS
    n_heads = q_ref.shape[-1] // hd
    n_pairs = n_heads // 2
    pair_lanes = [slice(p * LANES, (p + 1) * LANES) for p in range(n_pairs)]
    gate_rows = 3 * nb

    @pl.when(i == 0)
    def _():
        lane = lax.broadcasted_iota(jnp.int32, (1, LANES), 1)
        first = lane < hd
        for n in range(nb):
            kb = k_ref[0, n * bs:(n + 1) * bs, :]
            kmean_ref[n:n + 1, :] = jnp.sum(kb.astype(F32), axis=0, keepdims=True) * (1.0 / bs)
            for p in range(n_pairs):
                kp = kb[:, pair_lanes[p]]
                zero = jnp.zeros_like(kp)
                for h, kh in enumerate((jnp.where(first, kp, zero), jnp.where(first, zero, kp))):
                    rows = slice((2 * n + h) * bs, (2 * n + h + 1) * bs)
                    kaug_ref[p, rows, :] = jnp.concatenate([kh, kaux_ref[h]], axis=1)
        pieces = _split3(kmean_ref[...])
        for p in range(n_pairs):
            for h in range(2):
                for r, piece in enumerate(pieces):
                    pp = piece[:, pair_lanes[p]]
                    zero = jnp.zeros_like(pp)
                    ph = jnp.where(first, pp, zero) if h == 0 else jnp.where(first, zero, pp)
                    rows = slice(h * gate_rows + r * nb, h * gate_rows + (r + 1) * nb)
                    kms_ref[p, rows, :] = jnp.concatenate([ph, zero], axis=1)

    qaug_t = [jnp.concatenate([q_ref[0, :, pair_lanes[p]].astype(F32), qaux_ref[p].astype(F32)], axis=1).T.astype(BF16)
              for p in range(n_pairs)]
    sig = [sig_ref[hg * n_heads + h] for h in range(n_heads)]

    def select_blocks(h, g):
        nidx = lax.broadcasted_iota(jnp.int32, (nb, bs), 0)
        g = jnp.where(nidx < i, g, NEG_INF)
        sel = jnp.zeros((nb, bs), jnp.bool_)
        for _ in range(MOBA_TOPK):
            best = jnp.max(g, axis=0, keepdims=True)
            first = jnp.min(jnp.where(g == best, nidx, nb), axis=0, keepdims=True)
            pick = nidx == first
            sel = sel | pick
            g = jnp.where(pick, NEG_INF, g)
        sel_ref[h] = jnp.where(sel & (nidx < i), 0.0, NEG_INF)

    def stage_a(blk, slot, p, with_gate=False):
        start = pl.multiple_of(blk * (2 * bs), 2 * bs)
        lhs = kaug_ref[p, pl.ds(start, 2 * bs), :]
        if with_gate:
            lhs = jnp.concatenate([lhs, kms_ref[p]], axis=0)
        t = _dot(lhs, qaug_t[p])
        for h in range(2):
            th = t[h * bs:(h + 1) * bs]
            s_ref[slot, 2 * p + h] = th
            if with_gate:
                g0 = 2 * bs + h * gate_rows
                select_blocks(2 * p + h, t[g0:g0 + nb] + t[g0 + nb:g0 + 2 * nb] + t[g0 + 2 * nb:g0 + 3 * nb])
            else:
                mx_ref[slot, 2 * p + h:2 * p + h + 1, :] = jnp.max(th, axis=0, keepdims=True)

    def values(blk, h):
        start = pl.multiple_of(blk * bs, bs)
        return vt_ref[0, h * va:(h + 1) * va, pl.ds(start, bs)]

    def stage_b_own(slot, p):
        krow = lax.broadcasted_iota(jnp.int32, (bs, bs), 0)
        qcol = lax.broadcasted_iota(jnp.int32, (bs, bs), 1)
        causal = qcol >= krow
        for h in (2 * p, 2 * p + 1):
            t = jnp.where(causal, s_ref[slot, h], NEG_INF)
            m = jnp.max(t, axis=0, keepdims=True)
            pr = jnp.exp2(t - m)
            m_ref[h:h + 1, :] = m
            acc_ref[h] = _dot(values(i, h), pr.astype(BF16))

    def stage_b_past(blk, slot, p):
        off = ((i - blk) * bs).astype(F32)
        for h in (2 * p, 2 * p + 1):
            cs = sel_ref[h, pl.ds(blk, 1), :] - sig[h] * off
            m_old = m_ref[h:h + 1, :]
            m_new = jnp.maximum(m_old, mx_ref[slot, h:h + 1, :] + cs)
            alpha = jnp.exp2(m_old - m_new)
            pr = jnp.exp2(s_ref[slot, h] + (cs - m_new))
            m_ref[h:h + 1, :] = m_new
            acc_ref[h] = alpha * acc_ref[h] + _dot(values(blk, h), pr.astype(BF16))

    for p in range(n_pairs):
        stage_a(i, 0, p, with_gate=True)
    for p in range(n_pairs):
        stage_a(jnp.maximum(i - 1, 0), 1, p)
        stage_b_own(0, p)

    def step(u, carry):
        blk = i - 1 - 2 * u
        for p in range(n_pairs):
            stage_a(blk - 1, 0, p)
            stage_b_past(blk, 1, p)
        for p in range(n_pairs):
            stage_a(jnp.maximum(blk - 2, 0), 1, p)
            stage_b_past(blk - 1, 0, p)
        return carry

    lax.fori_loop(0, i // 2, step, 0)

    @pl.when(i % 2 == 1)
    def _():
        for p in range(n_pairs):
            stage_b_past(0, 1, p)

    for p in range(n_pairs):
        ot = jnp.concatenate([acc_ref[2 * p + h, :hd, :] / acc_ref[2 * p + h, hd:hd + 1, :] for h in range(2)], axis=0)
        o_ref[0, :, pair_lanes[p]] = ot.T.astype(BF16)


def _moba(qa, ka, vat, cast_weights=(), heads_per_step=8):
    B, S, W = qa.shape
    bs = MOBA_BLOCK
    nb = S // bs
    n_heads = W // MOBA_HEAD_DIM
    gw = heads_per_step * MOBA_HEAD_DIM
    sig, qaux, kaux = _alibi_tables(n_heads, bs)
    n_steps = B * (W // gw) * nb
    cast_specs = [pl.BlockSpec((w.shape[0] // n_steps, w.shape[1]), lambda b, hg, i: ((b * (W // gw) + hg) * nb + i, 0))
                  for w in cast_weights]
    outs = pl.pallas_call(
        functools.partial(_moba_kernel, len(cast_weights)),
        grid=(B, W // gw, nb),
        in_specs=[
            pl.BlockSpec(memory_space=pltpu.SMEM),
            pl.BlockSpec((1, bs, gw), lambda b, hg, i: (b, i, hg)),
            pl.BlockSpec((1, S, gw), lambda b, hg, i: (b, 0, hg)),
            pl.BlockSpec((1, heads_per_step * MOBA_VALUE_ROWS, S), lambda b, hg, i: (b, hg, 0)),
            pl.BlockSpec((heads_per_step // 2, bs, LANES), lambda b, hg, i: (hg, 0, 0)),
            pl.BlockSpec((2, bs, LANES), lambda b, hg, i: (0, 0, 0)),
        ] + cast_specs,
        out_specs=[pl.BlockSpec((1, bs, gw), lambda b, hg, i: (b, i, hg))] + cast_specs,
        out_shape=[jax.ShapeDtypeStruct((B, S, W), BF16)] + [jax.ShapeDtypeStruct(w.shape, BF16) for w in cast_weights],
        scratch_shapes=[
            pltpu.VMEM((nb, gw), F32),
            pltpu.VMEM((heads_per_step // 2, 2 * S, 2 * LANES), BF16),
            pltpu.VMEM((heads_per_step // 2, 6 * nb, 2 * LANES), BF16),
            pltpu.VMEM((heads_per_step, nb, bs), F32),
            pltpu.VMEM((heads_per_step, MOBA_VALUE_ROWS, bs), F32),
            pltpu.VMEM((8, bs), F32),
            pltpu.VMEM((2, heads_per_step, bs, bs), F32),
            pltpu.VMEM((2, 8, bs), F32),
        ],
        compiler_params=pltpu.CompilerParams(
            dimension_semantics=("arbitrary", "arbitrary", "arbitrary"), vmem_limit_bytes=VMEM_LIMIT_BYTES),
        name="moba_attention",
    )(sig, qa, ka, vat, qaux, kaux, *cast_weights)
    return outs[0], outs[1:]


def _gla_kernel(q_ref, k_ref, la_ref, v_ref, r_ref, gn_ref, o_ref, st_ref):
    s_idx = pl.program_id(1)
    C = GLA_CHUNK
    tc = q_ref.shape[1]

    @pl.when(s_idx == 0)
    def _():
        st_ref[...] = jnp.zeros_like(st_ref)

    G = GLA_GROUP
    n_pairs = GLA_HEADS // 2
    lane = lax.broadcasted_iota(jnp.int32, (1, LANES), 1)
    head_lanes = (lane < GLA_KEY_DIM, lane >= GLA_KEY_DIM)
    ti = lax.broadcasted_iota(jnp.int32, (G, G), 0)
    si = lax.broadcasted_iota(jnp.int32, (G, G), 1)
    causal = (ti // C == si // C) & (si <= ti)
    tril = jnp.where(causal, 1.0, 0.0).astype(BF16)
    gn = gn_ref[...]

    cpg = G // C
    n_groups = tc // G
    grp_rows = [slice(g * G, (g + 1) * G) for g in range(n_groups)]
    chunk_rows = [slice(c * C, (c + 1) * C) for c in range(cpg)]
    pair_lanes = [slice(hp * LANES, (hp + 1) * LANES) for hp in range(n_pairs)]
    val_lanes = [slice(hd * GLA_VAL_DIM, (hd + 1) * GLA_VAL_DIM) for hd in range(GLA_HEADS)]

    b = [sum(_dot(tril, piece) for piece in _split3(la_ref[0, rows, :])) for rows in grp_rows]

    sub = GLA_SUB
    spc = C // sub
    kw = q_ref.shape[-1]
    sub_of_row = (lax.broadcasted_iota(jnp.int32, (G, 1), 0) % C) // sub
    q_m, q_sub, k_sub, k_dec, decay = [], [], [], [], []
    for g, rows in enumerate(grp_rows):
        bg = b[g]
        b_last = [bg[c * C + C - 1:c * C + C, :] for c in range(cpg)]
        b_end = jnp.concatenate([jnp.broadcast_to(bl, (C, kw)) for bl in b_last], axis=0)
        zero_row = jnp.zeros_like(b_last[0])
        ref = [[zero_row if v == 0 else bg[c * C + v * sub - 1:c * C + v * sub, :] for v in range(spc)]
               for c in range(cpg)]
        ref_own = jnp.concatenate([jnp.broadcast_to(ref[c][v], (sub, kw)) for c in range(cpg) for v in range(spc)],
                                  axis=0)
        qs = q_ref[0, rows, :]
        ks = k_ref[0, rows, :]
        q_dec = qs * jnp.exp(bg)
        q_in = qs * jnp.exp(bg - ref_own)
        k_variants = []
        for v in range(spc):
            ref_v = jnp.concatenate([jnp.broadcast_to(ref[c][v], (C, kw)) for c in range(cpg)], axis=0)
            k_variants.append(jnp.where(sub_of_row <= v, ks * jnp.exp(ref_v - bg), 0.0).astype(BF16))
        k_sub.append(k_variants)
        k_dec.append((ks * jnp.exp(b_end - bg)).astype(BF16))
        decay.append([jnp.exp(bl) for bl in b_last])
        q_m.append([jnp.where(head_lanes[hd % 2], q_dec[:, pair_lanes[hd // 2]], 0.0).astype(BF16)
                    for hd in range(GLA_HEADS)])
        q_sub.append([[jnp.where(head_lanes[hd % 2] & (sub_of_row == v), q_in[:, pair_lanes[hd // 2]], 0.0).astype(BF16)
                       for v in range(spc)] for hd in range(GLA_HEADS)])

    def intra_scores(g, hd):
        lanes = pair_lanes[hd // 2]
        return sum(_dot_nt(jnp.concatenate([q_sub[g][hd][v], q_sub[g][hd][v + 1]], axis=1),
                           jnp.concatenate([k_sub[g][v][:, lanes], k_sub[g][v + 1][:, lanes]], axis=1))
                   for v in range(0, spc, 2))

    attn = [[intra_scores(g, hd) for hd in range(GLA_HEADS)]
            for g in range(n_groups)]

    update = [[[None] * cpg for _ in range(n_pairs)] for _ in range(n_groups)]
    for g, rows in enumerate(grp_rows):
        for hp in range(n_pairs):
            v_pair = v_ref[0, rows, hp * 2 * GLA_VAL_DIM:(hp + 1) * 2 * GLA_VAL_DIM]
            for c, cr in enumerate(chunk_rows):
                u = _dot_tn(v_pair[cr], k_dec[g][cr, pair_lanes[hp]])
                update[g][hp][c] = jnp.where(head_lanes[0], u[:GLA_VAL_DIM], u[GLA_VAL_DIM:])

    o_intra = [[_dot(jnp.where(causal, attn[g][hd], 0.0).astype(BF16), v_ref[0, grp_rows[g], val_lanes[hd]])
                for hd in range(GLA_HEADS)] for g in range(n_groups)]

    states = [st_ref[hp] for hp in range(n_pairs)]
    for g, rows in enumerate(grp_rows):
        o_inter = [[None] * cpg for _ in range(GLA_HEADS)]
        for c, cr in enumerate(chunk_rows):
            for hp in range(n_pairs):
                q_pair = jnp.concatenate([q_m[g][2 * hp][cr], q_m[g][2 * hp + 1][cr]], axis=0)
                oi = _dot_nt(q_pair, states[hp].astype(BF16))
                o_inter[2 * hp][c] = oi[:C]
                o_inter[2 * hp + 1][c] = oi[C:]
                states[hp] = states[hp] * decay[g][c][:, pair_lanes[hp]] + update[g][hp][c]
        for hd in range(GLA_HEADS):
            o = jnp.concatenate(o_inter[hd], axis=0) + o_intra[g][hd]
            ms = jnp.mean(o * o, axis=-1, keepdims=True)
            o_ref[0, rows, val_lanes[hd]] = (o * lax.rsqrt(ms + RMS_EPS) * gn * r_ref[0, rows, val_lanes[hd]]).astype(BF16)

    for hp in range(n_pairs):
        st_ref[hp] = states[hp]


def _gla(qg, kg, la, vg, rg, gnorm, tc=1024):
    B, S, kw = qg.shape
    vw = vg.shape[-1]
    row = lambda w: pl.BlockSpec((1, tc, w), lambda b, s: (b, s, 0))
    return pl.pallas_call(
        _gla_kernel,
        grid=(B, S // tc),
        in_specs=[row(kw), row(kw), row(kw), row(vw), row(vw),
                  pl.BlockSpec((1, GLA_VAL_DIM), lambda b, s: (0, 0))],
        out_specs=row(vw),
        out_shape=jax.ShapeDtypeStruct((B, S, vw), BF16),
        scratch_shapes=[pltpu.VMEM((GLA_HEADS // 2, GLA_VAL_DIM, LANES), F32)],
        compiler_params=pltpu.CompilerParams(
            dimension_semantics=("arbitrary", "arbitrary"), vmem_limit_bytes=VMEM_LIMIT_BYTES),
        name="gla_scan",
    )(qg, kg, la, vg, rg, gnorm)


def _post_mixer_kernel(alpha, x_ref, mo_ref, go_ref, wm_ref, g1_ref, b1_ref, wxq_ref,
                       mem_ref, wkv_ref, wxo_ref, g2_ref, b2_ref, o_ref, km_ref, vm_ref):
    @pl.when(pl.program_id(1) == 0)
    def _():
        kv = _dot(mem_ref[0].astype(BF16), wkv_ref[...])
        km_ref[...] = kv[:, :km_ref.shape[-1]].astype(BF16)
        vm_ref[...] = kv[:, km_ref.shape[-1]:].astype(BF16)

    tm, d = x_ref.shape[1], x_ref.shape[2]
    hd = d // XATTN_HEADS
    subs = [slice(r * POST_SUB_ROWS, (r + 1) * POST_SUB_ROWS) for r in range(tm // POST_SUB_ROWS)]
    heads = [slice(h * hd, (h + 1) * hd) for h in range(XATTN_HEADS)]
    aw = mo_ref.shape[-1]
    mix = [_dot(mo_ref[0, r, :], wm_ref[:aw, :]) + _dot(go_ref[0, r, :], wm_ref[aw:, :]) for r in subs]
    x1 = [_layer_norm(alpha * x_ref[0, r, :] + mix[i], g1_ref[...], b1_ref[...]) for i, r in enumerate(subs)]
    q = [(_dot(x1[i].astype(BF16), wxq_ref[...]) * (hd ** -0.5)).astype(BF16) for i in range(len(subs))]
    s = [[_dot_nt(q[i][:, c], km_ref[:, c]) for c in heads] for i in range(len(subs))]
    oh = []
    for i in range(len(subs)):
        row = []
        for h, c in enumerate(heads):
            m = jnp.max(s[i][h], axis=-1, keepdims=True)
            p = jnp.exp(s[i][h] - m)
            l = jnp.sum(p, axis=-1, keepdims=True)
            row.append((_dot(p.astype(BF16), vm_ref[:, c]) / l).astype(BF16))
        oh.append(row)
    for i, r in enumerate(subs):
        xa = sum(_dot(oh[i][h], wxo_ref[c, :]) for h, c in enumerate(heads))
        o_ref[0, r, :] = _layer_norm(alpha * x1[i] + xa, g2_ref[...], b2_ref[...])


def _post_mixer(x, moba_o, gla_o, w_mix_o, ln1_g, ln1_b, w_xq, mem, w_xkv, w_xo, ln2_g, ln2_b, alpha, tm=1024):
    B, S, D = x.shape
    aw = moba_o.shape[-1]
    M = mem.shape[1]
    wm = w_mix_o.astype(BF16)
    wxq = w_xq.astype(BF16)
    wxo = w_xo.astype(BF16)
    wkv = w_xkv.astype(BF16)
    vec = lambda a: a.reshape(1, D)
    row = lambda w: pl.BlockSpec((1, tm, w), lambda b, s: (b, s, 0))
    full = lambda a: pl.BlockSpec(a.shape, lambda b, s: (0,) * a.ndim)
    vspec = pl.BlockSpec((1, D), lambda b, s: (0, 0))
    memspec = pl.BlockSpec((1, M, D), lambda b, s: (b, 0, 0))
    return pl.pallas_call(
        functools.partial(_post_mixer_kernel, alpha),
        grid=(B, S // tm),
        in_specs=[row(D), row(aw), row(gla_o.shape[-1]), full(wm), vspec, vspec, full(wxq),
                  memspec, pl.BlockSpec(wkv.shape, lambda b, s: (0, 0), pipeline_mode=pl.Buffered(1)),
                  full(wxo), vspec, vspec],
        out_specs=row(D),
        out_shape=jax.ShapeDtypeStruct((B, S, D), F32),
        scratch_shapes=[pltpu.VMEM((M, D), BF16), pltpu.VMEM((M, D), BF16)],
        compiler_params=pltpu.CompilerParams(
            dimension_semantics=("arbitrary", "arbitrary"), vmem_limit_bytes=VMEM_LIMIT_BYTES),
        name="post_mixer",
    )(x, moba_o, gla_o, wm, vec(ln1_g), vec(ln1_b), wxq, mem, wkv, wxo, vec(ln2_g), vec(ln2_b))


def _mlp_kernel(alpha, fc, x_ref, w1_ref, w2_ref, g_ref, b_ref, o_ref):
    for r in range(x_ref.shape[1] // MLP_SUB_ROWS):
        rows = slice(r * MLP_SUB_ROWS, (r + 1) * MLP_SUB_ROWS)
        x = x_ref[0, rows, :]
        xb = x.astype(BF16)
        acc = jnp.zeros_like(x)
        for c in range(w1_ref.shape[1] // fc):
            cols = slice(c * fc, (c + 1) * fc)
            h = jnp.maximum(_dot(xb, w1_ref[:, cols]), 0.0)
            acc = acc + _dot((h * h).astype(BF16), w2_ref[cols, :])
        o_ref[0, rows, :] = _layer_norm(alpha * x + acc, g_ref[...], b_ref[...])


def _mlp(x, w_ff1, w_ff2, g, b, alpha, tm=1024, fc=512):
    B, S, D = x.shape
    w1 = w_ff1.astype(BF16)
    w2 = w_ff2.astype(BF16)
    row = pl.BlockSpec((1, tm, D), lambda bb, s: (bb, s, 0))
    full = lambda a: pl.BlockSpec(a.shape, lambda bb, s: (0, 0), pipeline_mode=pl.Buffered(1))
    vspec = pl.BlockSpec((1, D), lambda bb, s: (0, 0))
    return pl.pallas_call(
        functools.partial(_mlp_kernel, alpha, fc),
        grid=(B, S // tm),
        in_specs=[row, full(w1), full(w2), vspec, vspec],
        out_specs=row,
        out_shape=jax.ShapeDtypeStruct((B, S, D), F32),
        compiler_params=pltpu.CompilerParams(
            dimension_semantics=("arbitrary", "arbitrary"), vmem_limit_bytes=VMEM_LIMIT_BYTES),
        name="sq_relu_mlp",
    )(x, w1, w2, g.reshape(1, D), b.reshape(1, D))


def kernel(x, mem, w_in, w_gate_up, b_gate, gla_norm_g, w_mix_o, ln1_g, ln1_b, w_xq, w_xkv, w_xo,
           ln2_g, ln2_b, w_ff1, w_ff2, ln3_g, ln3_b):
    depth = w_in.shape[0]
    alpha = (2.0 * depth) ** 0.25
    for l in range(depth):
        qa, ka, vat, qg, kg, vg, rg, la = _in_projection(x, w_in[l], w_gate_up[l], b_gate[l])
        moba_o, (wkv, wmix, wxq, wxo, wf1, wf2) = _moba(
            qa, ka, vat, cast_weights=(w_xkv[l], w_mix_o[l], w_xq[l], w_xo[l], w_ff1[l], w_ff2[l]))
        gla_o = _gla(qg, kg, la, vg, rg, gla_norm_g[l].reshape(1, GLA_VAL_DIM))
        x = _post_mixer(x, moba_o, gla_o, wmix, ln1_g[l], ln1_b[l], wxq, mem, wkv, wxo, ln2_g[l], ln2_b[l], alpha)
        x = _mlp(x, wf1, wf2, ln3_g[l], ln3_b[l], alpha)
    return x
```
